```python
import functools
import jax, jax.numpy as jnp
from jax import lax
import numpy as np

D_MODEL = 1024
BATCH = 8
SEQ = 2048
DEPTH = 4
DEC_BATCH = 32
DEC_SEQ = 8
PAST_LEN = 16384
PAGE_SIZE = 128

A_HEADS = 4
A_HEAD_DIM = 64
A_WIDTH = A_HEADS * A_HEAD_DIM
A_ROPE_DIM = A_HEAD_DIM // 4
ROPE_THETA = 500000.0
MOBA_BLOCK = 256
MOBA_TOPK = 3
B_HEADS = 4
B_NOPE_DIM = 128
B_ROPE_DIM = 64
B_V_DIM = 128
B_WIDTH = B_HEADS * B_V_DIM
B_Q_RANK = 256
B_KV_RANK = 256
MLA_ROPE_THETA = 10000.0
MLA_SCALE = (B_NOPE_DIM + B_ROPE_DIM) ** -0.5
C_HEADS = 4
C_HEAD_DIM = 64
C_WIDTH = C_HEADS * C_HEAD_DIM
C_DECAY_RANK = 64
C_A_RANK = 64
C_SHIFT_WIDTH = 3 * C_WIDTH + C_DECAY_RANK + C_A_RANK
GN_EPS = 64e-5
MIX_WIDTH = A_WIDTH + B_WIDTH + C_WIDTH
A_COLS = 4 * A_WIDTH
B_COLS = B_Q_RANK + B_KV_RANK + B_ROPE_DIM + B_WIDTH
C_COLS = C_SHIFT_WIDTH + C_WIDTH
IN_COLS = A_COLS + B_COLS + C_COLS
Q_BLOCK = 128
DEEPNORM_ALPHA = (2 * DEPTH) ** 0.25
DEEPNORM_BETA = (8 * DEPTH) ** -0.25
LN_EPS = 1e-5
RMS_EPS = 1e-6

kernel_name = 'hymba_moba_mla_rwkv7_adaln_deepnorm_step'


def _split(x, sizes):
    return jnp.split(x, np.cumsum(sizes)[:-1].tolist(), axis=-1)


def layer_norm(x, g, b):
    xf = x.astype(jnp.float32)
    mu = jnp.mean(xf, -1, keepdims=True)
    var = jnp.mean(jnp.square(xf - mu), -1, keepdims=True)
    return ((xf - mu) * lax.rsqrt(var + LN_EPS) * g + b).astype(x.dtype)


def rms_norm(x, g):
    xf = x.astype(jnp.float32)
    return (xf * lax.rsqrt(jnp.mean(jnp.square(xf), -1, keepdims=True) + RMS_EPS) * g).astype(x.dtype)


def rope(x, pos, rot_dim, theta):
    half = rot_dim // 2
    inv = jnp.power(theta, -jnp.arange(half, dtype=jnp.float32) / half)
    ang = pos.astype(jnp.float32)[:, None] * inv[None, :]
    cos = jnp.cos(ang)[:, None, :]
    sin = jnp.sin(ang)[:, None, :]
    xf = x.astype(jnp.float32)
    x1, x2 = xf[..., :half], xf[..., half:rot_dim]
    out = jnp.concatenate([x1 * cos - x2 * sin, x1 * sin + x2 * cos, xf[..., rot_dim:]], axis=-1)
    return out.astype(x.dtype)


def block_means(k_blk):
    return jnp.mean(k_blk.astype(jnp.float32), axis=-3).astype(k_blk.dtype)


def moba_chunk(q, qpos, k_blk, v_blk, k_mean):
    nb = k_blk.shape[0]
    qc = q.shape[0]
    n_sel = min(MOBA_TOPK, nb)
    own = qpos[0] // MOBA_BLOCK
    scale = A_HEAD_DIM ** -0.5
    gate = jnp.einsum('qhd,nhd->qhn', q, k_mean).astype(jnp.float32)
    past = jnp.arange(nb, dtype=jnp.int32) < own
    gate = jnp.where(past[None, None, :], gate, -jnp.inf)
    _, sel = lax.top_k(gate, n_sel)
    valid = sel < own
    hidx = jnp.arange(A_HEADS)[None, :, None]
    k_g = jnp.transpose(k_blk, (2, 0, 1, 3))[hidx, sel]
    v_g = jnp.transpose(v_blk, (2, 0, 1, 3))[hidx, sel]
    s_sel = jnp.einsum('qhd,qhkpd->qhkp', q, k_g).astype(jnp.float32) * scale
    s_sel = jnp.where(valid[..., None], s_sel, -jnp.inf).reshape(qc, A_HEADS, n_sel * MOBA_BLOCK)
    k_own = lax.dynamic_index_in_dim(k_blk, own, 0, keepdims=False)
    v_own = lax.dynamic_index_in_dim(v_blk, own, 0, keepdims=False)
    own_pos = own * MOBA_BLOCK + jnp.arange(MOBA_BLOCK, dtype=jnp.int32)
    s_own = jnp.einsum('qhd,phd->qhp', q, k_own).astype(jnp.float32) * scale
    s_own = jnp.where((own_pos[None, :] <= qpos[:, None])[:, None, :], s_own, -jnp.inf)
    p = jax.nn.softmax(jnp.concatenate([s_sel, s_own], axis=-1), axis=-1).astype(q.dtype)
    p_sel = p[..., :n_sel * MOBA_BLOCK].reshape(qc, A_HEADS, n_sel, MOBA_BLOCK)
    p_own = p[..., n_sel * MOBA_BLOCK:]
    return jnp.einsum('qhkp,qhkpd->qhd', p_sel, v_g) + jnp.einsum('qhp,phd->qhd', p_own, v_own)


def moba_prompt(q, k, v):
    b_, s_ = q.shape[:2]
    nb = -(-s_ // MOBA_BLOCK)
    pad = nb * MOBA_BLOCK - s_
    k_blk = jnp.pad(k, ((0, 0), (0, pad), (0, 0), (0, 0))).reshape(b_, nb, MOBA_BLOCK, A_HEADS, A_HEAD_DIM)
    v_blk = jnp.pad(v, ((0, 0), (0, pad), (0, 0), (0, 0))).reshape(b_, nb, MOBA_BLOCK, A_HEADS, A_HEAD_DIM)
    k_mean = block_means(k_blk)
    nq = s_ // Q_BLOCK
    qb = q.reshape(b_, nq, Q_BLOCK, A_HEADS, A_HEAD_DIM)

    def one(i):
        bi, ci = i // nq, i % nq
        qpos = ci * Q_BLOCK + jnp.arange(Q_BLOCK, dtype=jnp.int32)
        return moba_chunk(qb[bi, ci], qpos, k_blk[bi], v_blk[bi], k_mean[bi])

    out = lax.map(one, jnp.arange(b_ * nq, dtype=jnp.int32))
    return out.reshape(b_, s_, A_HEADS, A_HEAD_DIM)


def moba_sample(q, k_new, v_new, pool_k, pool_v, page_table, layer):
    t_ = q.shape[1]
    past = page_table.shape[1] * PAGE_SIZE
    total = past + t_
    nb = -(-total // MOBA_BLOCK)
    pad = nb * MOBA_BLOCK - total
    qpos = past + jnp.arange(t_, dtype=jnp.int32)

    def one(args):
        q_b, k_b, v_b, pt = args
        k_all = jnp.concatenate([pool_k[layer, pt].reshape(past, A_HEADS, A_HEAD_DIM), k_b], axis=0)
        v_all = jnp.concatenate([pool_v[layer, pt].reshape(past, A_HEADS, A_HEAD_DIM), v_b], axis=0)
        k_blk = jnp.pad(k_all, ((0, pad), (0, 0), (0, 0))).reshape(nb, MOBA_BLOCK, A_HEADS, A_HEAD_DIM)
        v_blk = jnp.pad(v_all, ((0, pad), (0, 0), (0, 0))).reshape(nb, MOBA_BLOCK, A_HEADS, A_HEAD_DIM)
        return moba_chunk(q_b, qpos, k_blk, v_blk, block_means(k_blk))

    return lax.map(one, (q, k_new, v_new, page_table))


def mla_dense(q_abs, q_rope, qpos, ckv, kr, kpos):
    s = (jnp.einsum('qhc,lc->qhl', q_abs, ckv) + jnp.einsum('qhr,lr->qhl', q_rope, kr)).astype(jnp.float32) * MLA_SCALE
    s = jnp.where((kpos[None, :] <= qpos[:, None])[:, None, :], s, -jnp.inf)
    p = jax.nn.softmax(s, axis=-1).astype(ckv.dtype)
    return jnp.einsum('qhl,lc->qhc', p, ckv)


def mla_prompt(q_abs, q_rope, ckv, kr):
    b_, s_ = q_abs.shape[:2]
    nq = s_ // Q_BLOCK
    kpos = jnp.arange(s_, dtype=jnp.int32)
    qa = q_abs.reshape(b_, nq, Q_BLOCK, B_HEADS, B_KV_RANK)
    qr = q_rope.reshape(b_, nq, Q_BLOCK, B_HEADS, B_ROPE_DIM)

    def one(i):
        bi, ci = i // nq, i % nq
        qpos = ci * Q_BLOCK + jnp.arange(Q_BLOCK, dtype=jnp.int32)
        return mla_dense(qa[bi, ci], qr[bi, ci], qpos, ckv[bi], kr[bi], kpos)

    out = lax.map(one, jnp.arange(b_ * nq, dtype=jnp.int32))
    return out.reshape(b_, s_, B_HEADS, B_KV_RANK)


def mla_sample(q_abs, q_rope, ckv_new, kr_new, pool_c, pool_r, page_table, layer):
    t_ = q_abs.shape[1]
    past = page_table.shape[1] * PAGE_SIZE
    qpos = past + jnp.arange(t_, dtype=jnp.int32)
    kpos = jnp.arange(past + t_, dtype=jnp.int32)

    def one(args):
        qa, qr, cn, rn, pt = args
        c_all = jnp.concatenate([pool_c[layer, pt].reshape(past, B_KV_RANK), cn], axis=0)
        r_all = jnp.concatenate([pool_r[layer, pt].reshape(past, B_ROPE_DIM), rn], axis=0)
        return mla_dense(qa, qr, qpos, c_all, r_all, kpos)

    return lax.map(one, (q_abs, q_rope, ckv_new, kr_new, page_table))


def rwkv7_mix(p, prev, state, mu, w0, w2, a0, a2, k_k, k_a, r_k, gn_g, gn_b):
    b_, l_ = p.shape[:2]
    f32 = jnp.float32
    shifted = jnp.concatenate([prev[:, None, :], p[:, :-1]], axis=1)
    m = p + mu * (shifted - p)
    r, k, v, wd, ad = _split(m, [C_WIDTH, C_WIDTH, C_WIDTH, C_DECAY_RANK, C_A_RANK])
    w_log = -jax.nn.softplus(-(w0 + jnp.tanh(wd) @ w2).astype(f32)) - 0.5
    decay = jnp.exp(-jnp.exp(w_log))
    a = jax.nn.sigmoid((a0 + ad @ a2).astype(f32))
    hs = lambda t: t.astype(f32).reshape(b_, l_, C_HEADS, C_HEAD_DIM)
    kk = hs(k * k_k)
    kk = kk / jnp.maximum(jnp.sqrt(jnp.sum(jnp.square(kk), -1, keepdims=True)), 1e-12)
    a_h = hs(a)
    k_a_h = k_a.astype(f32).reshape(C_HEADS, C_HEAD_DIM)
    k_h = hs(k) * (1.0 + (a_h - 1.0) * k_a_h)
    r_h, v_h, w_h = hs(r), hs(v), hs(decay)

    def step(s, inp):
        r_t, w_t, k_t, v_t, kk_t, a_t = inp
        sa = jnp.einsum('bhij,bhj->bhi', s, -kk_t)
        s = s * w_t[:, :, None, :] + sa[..., :, None] * (kk_t * a_t)[..., None, :] + v_t[..., :, None] * k_t[..., None, :]
        return s, jnp.einsum('bhij,bhj->bhi', s, r_t)

    xs = tuple(jnp.swapaxes(t, 0, 1) for t in (r_h, w_h, k_h, v_h, kk, a_h))
    s_fin, out = lax.scan(step, state.astype(f32), xs)
    out = jnp.swapaxes(out, 0, 1)
    mu_o = jnp.mean(out, -1, keepdims=True)
    var_o = jnp.mean(jnp.square(out - mu_o), -1, keepdims=True)
    out = ((out - mu_o) * lax.rsqrt(var_o + GN_EPS)).reshape(b_, l_, C_WIDTH) * gn_g + gn_b
    bonus = jnp.sum(r_h * k_h * r_k.astype(f32), -1, keepdims=True) * v_h
    o = (out + bonus.reshape(b_, l_, C_WIDTH)).astype(p.dtype)
    return o, s_fin.astype(state.dtype)


def mixer_layer(x, c, pos, shift_prev, wkv_state, moba_fn, mla_fn,
                w_ada, b_ada, w_in, q_norm, kv_norm, w_uq, w_uk, w_uv,
                mu, w0, w2, a0, a2, k_k, k_a, r_k, gn_g, gn_b, w_out, ln_g, ln_b):
    b_, l_ = x.shape[:2]
    shift, scale, gate = jnp.split(jax.nn.silu(c) @ w_ada + b_ada, 3, axis=-1)
    h = x * (1.0 + scale[:, None, :]) + shift[:, None, :]
    pa, pb, pc = _split(h @ w_in, [A_COLS, B_COLS, C_COLS])
    qa, ka, va, ga = jnp.split(pa, 4, axis=-1)
    qa = rope(qa.reshape(b_, l_, A_HEADS, A_HEAD_DIM), pos, A_ROPE_DIM, ROPE_THETA)
    ka = rope(ka.reshape(b_, l_, A_HEADS, A_HEAD_DIM), pos, A_ROPE_DIM, ROPE_THETA)
    va = va.reshape(b_, l_, A_HEADS, A_HEAD_DIM)
    oa = moba_fn(qa, ka, va).reshape(b_, l_, A_WIDTH)
    cq, ckv, kr, gb = _split(pb, [B_Q_RANK, B_KV_RANK, B_ROPE_DIM, B_WIDTH])
    qf = (rms_norm(cq, q_norm) @ w_uq).reshape(b_, l_, B_HEADS, B_NOPE_DIM + B_ROPE_DIM)
    q_nope = qf[..., :B_NOPE_DIM]
    q_rope = rope(qf[..., B_NOPE_DIM:], pos, B_ROPE_DIM, MLA_ROPE_THETA)
    ckv = rms_norm(ckv, kv_norm)
    kr = rope(kr[:, :, None, :], pos, B_ROPE_DIM, MLA_ROPE_THETA)[:, :, 0, :]
    q_abs = jnp.einsum('blhn,chn->blhc', q_nope, w_uk)
    ob = jnp.einsum('blhc,chv->blhv', mla_fn(q_abs, q_rope, ckv, kr), w_uv).reshape(b_, l_, B_WIDTH)
    pcs, gc = _split(pc, [C_SHIFT_WIDTH, C_WIDTH])
    oc, new_state = rwkv7_mix(pcs, shift_prev, wkv_state, mu, w0, w2, a0, a2, k_k, k_a, r_k, gn_g, gn_b)
    mixed = jnp.concatenate([oa * jax.nn.silu(ga), ob * jax.nn.silu(gb), oc * jax.nn.silu(gc)], axis=-1) @ w_out
    y = layer_norm(DEEPNORM_ALPHA * x + gate[:, None, :] * mixed, ln_g, ln_b)
    return y, ka, va, ckv, kr, new_state, pcs[:, -1]


def setup_inputs(seed: int = 0) -> dict:
    key = jax.random.key(seed)
    ks = jax.random.split(key, 36)
    f32 = jnp.float32
    nrm = lambda k, shape, s: s * jax.random.normal(k, shape, f32)
    n_pages = PAST_LEN // PAGE_SIZE
    n_pool = (DEC_BATCH * n_pages * 5) // 4
    d = D_MODEL
    page_table = jax.random.permutation(ks[8], n_pool)[:DEC_BATCH * n_pages].reshape(DEC_BATCH, n_pages).astype(jnp.int32)
    return {
        'x_prompt': nrm(ks[0], (BATCH, SEQ, d), 1.0),
        'x_sample': nrm(ks[1], (DEC_BATCH, DEC_SEQ, d), 1.0),
        'cache_moba_k': nrm(ks[2], (DEPTH, n_pool, PAGE_SIZE, A_HEADS, A_HEAD_DIM), 1.0),
        'cache_moba_v': nrm(ks[3], (DEPTH, n_pool, PAGE_SIZE, A_HEADS, A_HEAD_DIM), 1.0),
        'cache_mla_latent': nrm(ks[4], (DEPTH, n_pool, PAGE_SIZE, B_KV_RANK), 1.0),
        'cache_mla_rope': nrm(ks[5], (DEPTH, n_pool, PAGE_SIZE, B_ROPE_DIM), 1.0),
        'state_rwkv_wkv': nrm(ks[6], (DEPTH, DEC_BATCH, C_HEADS, C_HEAD_DIM, C_HEAD_DIM), 0.1),
        'state_rwkv_shift': nrm(ks[7], (DEPTH, DEC_BATCH, C_SHIFT_WIDTH), 1.0),
        'page_table': page_table,
        'c_prompt': nrm(ks[9], (BATCH, d), 1.0),
        'c_sample': nrm(ks[10], (DEC_BATCH, d), 1.0),
        'w_ada': nrm(ks[11], (DEPTH, d, 3 * d), 0.5 * d ** -0.5),
        'b_ada': nrm(ks[12], (DEPTH, 3 * d), 0.02),
        'w_in': nrm(ks[13], (DEPTH, d, IN_COLS), d ** -0.5),
        'mla_q_norm': 1.0 + nrm(ks[14], (DEPTH, B_Q_RANK), 0.01),
        'mla_kv_norm': 1.0 + nrm(ks[15], (DEPTH, B_KV_RANK), 0.01),
        'mla_w_uq': nrm(ks[16], (DEPTH, B_Q_RANK, B_HEADS * (B_NOPE_DIM + B_ROPE_DIM)), B_Q_RANK ** -0.5),
        'mla_w_uk': nrm(ks[17], (DEPTH, B_KV_RANK, B_HEADS, B_NOPE_DIM), B_KV_RANK ** -0.5),
        'mla_w_uv': nrm(ks[18], (DEPTH, B_KV_RANK, B_HEADS, B_V_DIM), B_KV_RANK ** -0.5),
        'rwkv_mu': jax.random.uniform(ks[19], (DEPTH, C_SHIFT_WIDTH), f32),
        'rwkv_w0': nrm(ks[20], (DEPTH, C_WIDTH), 0.5),
        'rwkv_w2': nrm(ks[21], (DEPTH, C_DECAY_RANK, C_WIDTH), 0.1),
        'rwkv_a0': nrm(ks[22], (DEPTH, C_WIDTH), 0.1),
        'rwkv_a2': nrm(ks[23], (DEPTH, C_A_RANK, C_WIDTH), 0.1),
        'rwkv_k_k': 0.85 + nrm(ks[24], (DEPTH, C_WIDTH), 0.02),
        'rwkv_k_a': 1.0 + nrm(ks[25], (DEPTH, C_WIDTH), 0.02),
        'rwkv_r_k': nrm(ks[26], (DEPTH, C_HEADS, C_HEAD_DIM), 0.1),
        'rwkv_gn_g': 1.0 + nrm(ks[27], (DEPTH, C_WIDTH), 0.01),
        'rwkv_gn_b': nrm(ks[28], (DEPTH, C_WIDTH), 0.01),
        'w_out': nrm(ks[29], (DEPTH, MIX_WIDTH, d), DEEPNORM_BETA * MIX_WIDTH ** -0.5),
        'ln_g': 1.0 + nrm(ks[30], (DEPTH, d), 0.01),
        'ln_b': nrm(ks[31], (DEPTH, d), 0.01),
    }


def reference(x_prompt, x_sample, cache_moba_k, cache_moba_v, cache_mla_latent, cache_mla_rope,
              state_rwkv_wkv, state_rwkv_shift, page_table, c_prompt, c_sample,
              w_ada, b_ada, w_in, mla_q_norm, mla_kv_norm, mla_w_uq, mla_w_uk, mla_w_uv,
              rwkv_mu, rwkv_w0, rwkv_w2, rwkv_a0, rwkv_a2, rwkv_k_k, rwkv_k_a, rwkv_r_k,
              rwkv_gn_g, rwkv_gn_b, w_out, ln_g, ln_b):
    pos_p = jnp.arange(x_prompt.shape[1], dtype=jnp.int32)
    past = page_table.shape[1] * PAGE_SIZE
    pos_s = past + jnp.arange(x_sample.shape[1], dtype=jnp.int32)
    zero_shift = jnp.zeros((x_prompt.shape[0], C_SHIFT_WIDTH), x_prompt.dtype)
    zero_state = jnp.zeros((x_prompt.shape[0], C_HEADS, C_HEAD_DIM, C_HEAD_DIM), x_prompt.dtype)
    xp, xs = x_prompt, x_sample
    pk, pv, pc, pr, pst, psh = [], [], [], [], [], []
    sk, sv, sc, sr, sst, ssh = [], [], [], [], [], []
    for l in range(DEPTH):
        lp = (w_ada[l], b_ada[l], w_in[l], mla_q_norm[l], mla_kv_norm[l], mla_w_uq[l], mla_w_uk[l], mla_w_uv[l],
              rwkv_mu[l], rwkv_w0[l], rwkv_w2[l], rwkv_a0[l], rwkv_a2[l], rwkv_k_k[l], rwkv_k_a[l], rwkv_r_k[l],
              rwkv_gn_g[l], rwkv_gn_b[l], w_out[l], ln_g[l], ln_b[l])
        xp, ka, va, cl, cr, st, sh = mixer_layer(xp, c_prompt, pos_p, zero_shift, zero_state,
                                                 moba_prompt, mla_prompt, *lp)
        pk.append(ka); pv.append(va); pc.append(cl); pr.append(cr); pst.append(st); psh.append(sh)
        moba_fn = functools.partial(moba_sample, pool_k=cache_moba_k, pool_v=cache_moba_v,
                                    page_table=page_table, layer=l)
        mla_fn = functools.partial(mla_sample, pool_c=cache_mla_latent, pool_r=cache_mla_rope,
                                   page_table=page_table, layer=l)
        xs, ka, va, cl, cr, st, sh = mixer_layer(xs, c_sample, pos_s, state_rwkv_shift[l], state_rwkv_wkv[l],
                                                 moba_fn, mla_fn, *lp)
        sk.append(ka); sv.append(va); sc.append(cl); sr.append(cr); sst.append(st); ssh.append(sh)
    return (xp, xs,
            jnp.stack(pk), jnp.stack(pv), jnp.stack(pc), jnp.stack(pr), jnp.stack(pst), jnp.stack(psh),
            jnp.stack(sk), jnp.stack(sv), jnp.stack(sc), jnp.stack(sr), jnp.stack(sst), jnp.stack(ssh))
```

```python
import functools
import math

import jax
import jax.numpy as jnp
from jax import lax
from jax.experimental import pallas as pl
from jax.experimental.pallas import tpu as pltpu

F32 = jnp.float32
BF16 = jnp.bfloat16
HI = lax.Precision.HIGHEST

D_MODEL = 1024
PAGE_SIZE = 128
LANES = 128
A_HEADS = 4
A_HEAD_DIM = 64
A_WIDTH = 256
A_ROPE_DIM = 16
ROPE_THETA = 500000.0
MOBA_BLOCK = 256
MOBA_TOPK = 3
Q_BLOCK = 128
B_HEADS = 4
B_NOPE_DIM = 128
B_ROPE_DIM = 64
B_V_DIM = 128
B_WIDTH = 512
B_Q_RANK = 256
B_KV_RANK = 256
MLA_ROPE_THETA = 10000.0
MLA_SCALE = (B_NOPE_DIM + B_ROPE_DIM) ** -0.5
C_HEADS = 4
C_HEAD_DIM = 64
C_WIDTH = 256
C_DECAY_RANK = 64
C_A_RANK = 64
GN_EPS = 64e-5
RWKV_CHUNK = 64
RWKV_SUB = 16
LN_EPS = 1e-5
RMS_EPS = 1e-6
NEG = -1e30

OFF_QA, OFF_KA, OFF_VA, OFF_GA = 0, 256, 512, 768
OFF_CQ, OFF_CKV, OFF_GB, OFF_RKV, OFF_GC, OFF_KR, OFF_WDAD = 1024, 1280, 1536, 2048, 2816, 3072, 3200
IN_COLS_PAD = 3328

VMEM_LIMIT = 56 * 1024 * 1024


def _cp(sem):
    return pltpu.CompilerParams(dimension_semantics=sem, vmem_limit_bytes=VMEM_LIMIT)


def _dot(a, b):
    return jnp.dot(a.astype(BF16), b.astype(BF16), preferred_element_type=F32)


def _dot_nt(a, b):
    return lax.dot_general(a.astype(BF16), b.astype(BF16), (((1,), (1,)), ((), ())),
                           preferred_element_type=F32)


def _doth(a, b):
    return jnp.dot(a, b, precision=HI, preferred_element_type=F32)


def _doth_nt(a, b):
    return lax.dot_general(a, b, (((1,), (1,)), ((), ())), precision=HI, preferred_element_type=F32)


def _sigmoid(x):
    return 1.0 / (1.0 + jnp.exp(-x))


def _silu(x):
    return x * _sigmoid(x)


def _head_mask(rows, width, head, head_dim):
    lane = lax.broadcasted_iota(jnp.int32, (rows, width), 1)
    return (lane >= head * head_dim) & (lane < (head + 1) * head_dim)


def _ada_kernel(c_ref, w_ref, b_ref, o_ref):
    o_ref[...] = _dot(_silu(c_ref[...]), w_ref[...]) + b_ref[...]


def _ada_all(c_all, w_ada, b_ada):
    depth = w_ada.shape[0]
    r = c_all.shape[0]
    nj = w_ada.shape[2] // D_MODEL
    return pl.pallas_call(
        _ada_kernel,
        grid=(depth, nj),
        in_specs=[pl.BlockSpec((r, D_MODEL), lambda l, j: (0, 0)),
                  pl.BlockSpec((None, D_MODEL, D_MODEL), lambda l, j: (l, 0, j)),
                  pl.BlockSpec((None, 1, D_MODEL), lambda l, j: (l, 0, j))],
        out_specs=pl.BlockSpec((None, r, D_MODEL), lambda l, j: (l, 0, j)),
        out_shape=jax.ShapeDtypeStruct((depth, r, w_ada.shape[2]), F32),
        compiler_params=_cp(("arbitrary", "arbitrary")),
        name="adaln_mod",
    )(c_all, w_ada, b_ada.reshape(depth, 1, -1))


def _rope_tables(pos, rot_half, theta, head_width, n_rep):
    inv = jnp.power(theta, -jnp.arange(rot_half, dtype=F32) / rot_half)
    ang = pos.astype(F32)[:, None] * inv[None, :]
    cos, sin = jnp.cos(ang), jnp.sin(ang)
    p = pos.shape[0]
    rest = head_width - 2 * rot_half
    c = jnp.concatenate([cos, cos, jnp.ones((p, rest), F32)], axis=1)
    s1 = jnp.concatenate([-sin, jnp.zeros((p, head_width - rot_half), F32)], axis=1)
    s2 = jnp.concatenate([jnp.zeros((p, rot_half), F32), sin, jnp.zeros((p, rest), F32)], axis=1)
    return jnp.stack([jnp.tile(t, (1, n_rep)) for t in (c, s1, s2)])


def _apply_rope(x, cos, s1, s2, half):
    w = x.shape[-1]
    return x * cos + pltpu.roll(x, w - half, 1) * s1 + pltpu.roll(x, half, 1) * s2


def _rms(x, g):
    return x * lax.rsqrt(jnp.mean(x * x, axis=-1, keepdims=True) + RMS_EPS) * g


def _in_kernel(x_ref, sc_ref, sh_ref, w_ref, wuq_ref, wuk_ref, qn_ref, kvn_ref, ta_ref, tb_ref,
               qa_ref, ka_ref, va_ref, ga_ref, qabs_ref, qrope_ref, ckv_ref, kr_ref, gb_ref,
               rkv_ref, wdad_ref, gc_ref):
    hb = (x_ref[...] * (1.0 + sc_ref[...]) + sh_ref[...]).astype(BF16)

    def proj(off, width):
        return jnp.dot(hb, w_ref[:, off:off + width], preferred_element_type=F32)

    ca, sa1, sa2 = ta_ref[0], ta_ref[1], ta_ref[2]
    cb, sb1, sb2 = tb_ref[0], tb_ref[1], tb_ref[2]
    qa_ref[...] = _apply_rope(proj(OFF_QA, A_WIDTH), ca, sa1, sa2, A_ROPE_DIM // 2)
    ka_ref[...] = _apply_rope(proj(OFF_KA, A_WIDTH), ca, sa1, sa2, A_ROPE_DIM // 2)
    va_ref[...] = proj(OFF_VA, A_WIDTH)
    ga_ref[...] = proj(OFF_GA, A_WIDTH)
    cqn = _rms(proj(OFF_CQ, B_Q_RANK), qn_ref[...])
    qf = _dot(cqn, wuq_ref[...])
    for h in range(B_HEADS):
        qabs_ref[:, h * B_KV_RANK:(h + 1) * B_KV_RANK] = _dot(
            qf[:, h * B_NOPE_DIM:(h + 1) * B_NOPE_DIM], wuk_ref[h])
        base = B_HEADS * B_NOPE_DIM + h * LANES
        qrope_ref[:, h * LANES:(h + 1) * LANES] = _apply_rope(
            qf[:, base:base + LANES], cb, sb1, sb2, B_ROPE_DIM // 2)
    ckv_ref[...] = _rms(proj(OFF_CKV, B_KV_RANK), kvn_ref[...])
    kr_ref[...] = _apply_rope(proj(OFF_KR, LANES), cb, sb1, sb2, B_ROPE_DIM // 2)[:, :B_ROPE_DIM]
    gb_ref[...] = proj(OFF_GB, B_WIDTH)
    rkv_ref[...] = proj(OFF_RKV, 3 * C_WIDTH)
    wdad_ref[...] = proj(OFF_WDAD, LANES)
    gc_ref[...] = proj(OFF_GC, C_WIDTH)


def _in_proj(x2d, scale, shift, tiles_per_mod, w, wuq, wuk, qn, kvn, tab_a, tab_b, tm):
    r = x2d.shape[0]
    n_tiles = r // tm
    tab_tiles = tab_a.shape[1] // tm
    rb = scale.shape[1]
    widths = [A_WIDTH, A_WIDTH, A_WIDTH, A_WIDTH, B_HEADS * B_KV_RANK, B_HEADS * LANES, B_KV_RANK,
              B_ROPE_DIM, B_WIDTH, 3 * C_WIDTH, LANES, C_WIDTH]
    row = lambda wd: pl.BlockSpec((tm, wd), lambda i: (i, 0))
    full = lambda a: pl.BlockSpec(a.shape, lambda i: (0,) * a.ndim)
    mod = pl.BlockSpec((None, rb, D_MODEL), lambda i: (i // tiles_per_mod, 0, 0))
    tab = lambda a: pl.BlockSpec((3, tm, a.shape[2]), lambda i: (0, i % tab_tiles, 0))
    return pl.pallas_call(
        _in_kernel,
        grid=(n_tiles,),
        in_specs=[row(D_MODEL), mod, mod, full(w), full(wuq), full(wuk), full(qn), full(kvn),
                  tab(tab_a), tab(tab_b)],
        out_specs=[row(wd) for wd in widths],
        out_shape=[jax.ShapeDtypeStruct((r, wd), F32) for wd in widths],
        compiler_params=_cp(("arbitrary",)),
        name="in_proj",
    )(x2d, scale, shift, w, wuq, wuk, qn, kvn, tab_a, tab_b)


def _top_select(g, valid, axis):
    n = g.shape[axis]
    idx = lax.broadcasted_iota(jnp.int32, g.shape, axis).astype(F32)
    g = jnp.where(valid, g, -jnp.inf)
    sel = jnp.zeros(g.shape, jnp.bool_)
    for _ in range(min(MOBA_TOPK, n)):
        mx = jnp.max(g, axis=axis, keepdims=True)
        first = jnp.min(jnp.where(g == mx, idx, float(n)), axis=axis, keepdims=True)
        hit = idx == first
        sel = sel | hit
        g = jnp.where(hit, -jnp.inf, g)
    return sel & valid


def _stack_heads_masked(q, n_heads, head_dim):
    r, w = q.shape
    return jnp.concatenate(
        [jnp.where(_head_mask(r, w, h, head_dim), q, 0.0) for h in range(n_heads)], axis=0)


def _merge_heads(acc, n_heads, head_dim):
    r = acc.shape[0] // n_heads
    w = acc.shape[1]
    out = jnp.zeros((r, w), F32)
    for h in range(n_heads):
        out = out + jnp.where(_head_mask(r, w, h, head_dim), acc[h * r:(h + 1) * r], 0.0)
    return out


def _softmax_step(s, m, l, acc, v):
    m_new = jnp.maximum(m, jnp.max(s, axis=-1, keepdims=True))
    alpha = jnp.exp(m - m_new)
    p = jnp.exp(s - m_new)
    l = alpha * l + jnp.sum(p, axis=-1, keepdims=True)
    acc = alpha * acc + _dot(p, v)
    return m_new, l, acc


def _moba_prompt_kernel(q_ref, k_ref, v_ref, o_ref, kmean_ref, *, nb):
    c = pl.program_id(1)
    own = (c * Q_BLOCK) // MOBA_BLOCK

    @pl.when(c == 0)
    def _():
        kmean_ref[...] = jnp.concatenate(
            [jnp.mean(k_ref[n * MOBA_BLOCK:(n + 1) * MOBA_BLOCK, :], axis=0, keepdims=True)
             for n in range(nb)], axis=0)

    scale = A_HEAD_DIM ** -0.5
    qm = _stack_heads_masked(q_ref[...], A_HEADS, A_HEAD_DIM)
    rows = A_HEADS * Q_BLOCK
    gate_t = _doth_nt(kmean_ref[...], qm)
    blk = lax.broadcasted_iota(jnp.int32, gate_t.shape, 0)
    sel_t = _top_select(gate_t, blk < own, 0).astype(F32)
    eye = (lax.broadcasted_iota(jnp.int32, (rows, rows), 0)
           == lax.broadcasted_iota(jnp.int32, (rows, rows), 1)).astype(F32)
    sel = _dot_nt(eye, sel_t)

    qpos = c * Q_BLOCK + lax.broadcasted_iota(jnp.int32, (rows, 1), 0) % Q_BLOCK
    qb = (qm * scale).astype(BF16)
    m = jnp.full((rows, 1), NEG, F32)
    l = jnp.zeros((rows, 1), F32)
    acc = jnp.zeros((rows, A_WIDTH), F32)
    for n in range(nb):
        kb = k_ref[n * MOBA_BLOCK:(n + 1) * MOBA_BLOCK, :]
        vb = v_ref[n * MOBA_BLOCK:(n + 1) * MOBA_BLOCK, :]
        s = _dot_nt(qb, kb)
        kpos = n * MOBA_BLOCK + lax.broadcasted_iota(jnp.int32, (1, MOBA_BLOCK), 1)
        allowed = (sel[:, n:n + 1] > 0.5) | ((qpos // MOBA_BLOCK == n) & (kpos <= qpos))
        s = jnp.where(allowed, s, NEG)
        m, l, acc = _softmax_step(s, m, l, acc, vb)
    o_ref[...] = _merge_heads(acc / l, A_HEADS, A_HEAD_DIM)


def _moba_prompt(q, k, v):
    b, s, w = q.shape
    nb = s // MOBA_BLOCK
    blk = pl.BlockSpec((None, Q_BLOCK, w), lambda i, c: (i, c, 0))
    seq = pl.BlockSpec((None, s, w), lambda i, c: (i, 0, 0))
    return pl.pallas_call(
        functools.partial(_moba_prompt_kernel, nb=nb),
        grid=(b, s // Q_BLOCK),
        in_specs=[blk, seq, seq],
        out_specs=blk,
        out_shape=jax.ShapeDtypeStruct((b, s, w), F32),
        scratch_shapes=[pltpu.VMEM((nb, w), F32)],
        compiler_params=_cp(("arbitrary", "arbitrary")),
        name="moba_prompt",
    )(q, k, v)


def _stack_mla_q(qabs_ref, qrope_ref):
    qs = jnp.concatenate([qabs_ref[:, h * B_KV_RANK:(h + 1) * B_KV_RANK] for h in range(B_HEADS)], axis=0)
    qr = jnp.concatenate([qrope_ref[:, h * LANES:h * LANES + B_ROPE_DIM] for h in range(B_HEADS)], axis=0)
    return (qs * MLA_SCALE).astype(BF16), (qr * MLA_SCALE).astype(BF16)


def _mla_finish(acc, l, wuv_ref, o_ref, r):
    o = acc / l
    for h in range(B_HEADS):
        o_ref[:, h * B_V_DIM:(h + 1) * B_V_DIM] = _dot(o[h * r:(h + 1) * r], wuv_ref[h])


def _mla_prompt_kernel(qabs_ref, qrope_ref, ckv_ref, kr_ref, wuv_ref, o_ref, *, tk):
    c = pl.program_id(1)
    rows = B_HEADS * Q_BLOCK
    qs, qr = _stack_mla_q(qabs_ref, qrope_ref)
    qpos = c * Q_BLOCK + lax.broadcasted_iota(jnp.int32, (rows, 1), 0) % Q_BLOCK
    n_chunks = (c * Q_BLOCK + Q_BLOCK + tk - 1) // tk

    def body(j, carry):
        m, l, acc = carry
        start = pl.multiple_of(j * tk, tk)
        kc = ckv_ref[pl.ds(start, tk), :].astype(BF16)
        rc = kr_ref[pl.ds(start, tk), :].astype(BF16)
        s = _dot_nt(qs, kc) + _dot_nt(qr, rc)
        kpos = j * tk + lax.broadcasted_iota(jnp.int32, (1, tk), 1)
        s = jnp.where(kpos <= qpos, s, NEG)
        return _softmax_step(s, m, l, acc, kc)

    m0 = jnp.full((rows, 1), NEG, F32)
    l0 = jnp.zeros((rows, 1), F32)
    a0 = jnp.zeros((rows, B_KV_RANK), F32)
    m, l, acc = lax.fori_loop(0, n_chunks, body, (m0, l0, a0))
    _mla_finish(acc, l, wuv_ref, o_ref, Q_BLOCK)


def _mla_prompt(qabs, qrope, ckv, kr, wuv):
    b, s, _ = qabs.shape
    tk = min(256, s)
    blk = lambda w: pl.BlockSpec((None, Q_BLOCK, w), lambda i, c: (i, c, 0))
    seq = lambda w: pl.BlockSpec((None, s, w), lambda i, c: (i, 0, 0))
    return pl.pallas_call(
        functools.partial(_mla_prompt_kernel, tk=tk),
        grid=(b, s // Q_BLOCK),
        in_specs=[blk(qabs.shape[2]), blk(qrope.shape[2]), seq(B_KV_RANK), seq(B_ROPE_DIM),
                  pl.BlockSpec(wuv.shape, lambda i, c: (0, 0, 0))],
        out_specs=blk(B_WIDTH),
        out_shape=jax.ShapeDtypeStruct((b, s, B_WIDTH), F32),
        compiler_params=_cp(("arbitrary", "arbitrary")),
        name="mla_prompt",
    )(qabs, qrope, ckv, kr, wuv)


def _neumann(a, eye, levels):
    t = eye - a
    p = a
    for _ in range(levels - 1):
        p = _doth(p, p)
        t = _doth(t, eye + p)
    return t


def _unit_lower_inverse(a, eye, chunk, row_t, col_t):
    if chunk <= RWKV_SUB:
        return _neumann(a, eye, max(1, math.ceil(math.log2(chunk))))
    near = (row_t // RWKV_SUB) == (col_t // RWKV_SUB)
    a_d = jnp.where(near, a, 0.0)
    t_d = _neumann(a_d, eye, int(math.log2(RWKV_SUB)))
    n = _doth(t_d, a - a_d)
    t_n = _neumann(n, eye, max(1, math.ceil(math.log2(chunk // RWKV_SUB))))
    return _doth(t_n, t_d)


def _rwkv_kernel(rkv_ref, wdad_ref, prkv_ref, pwdad_ref, s0_ref, mu_rkv_ref, mu_wdad_ref,
                 w0_ref, w2_ref, a0_ref, a2_ref, kk_ref, ka_ref, rk_ref, gng_ref, gnb_ref,
                 o_ref, s_ref, *, chunk, n_chunks):
    h_, d_, w_ = C_HEADS, C_HEAD_DIM, C_WIDTH
    n = h_ * chunk
    s_ref[...] = s0_ref[...]

    ri = lax.broadcasted_iota(jnp.int32, (n, n), 0)
    ci = lax.broadcasted_iota(jnp.int32, (n, n), 1)
    same = (ri // chunk) == (ci // chunk)
    row_t, col_t = ri % chunk, ci % chunk
    strict = (same & (row_t > col_t))
    incl = (same & (row_t >= col_t))
    eye = (ri == ci).astype(F32)
    tri = (lax.broadcasted_iota(jnp.int32, (chunk, chunk), 0)
           >= lax.broadcasted_iota(jnp.int32, (chunk, chunk), 1)).astype(F32)
    seg = ((lax.broadcasted_iota(jnp.int32, (w_, w_), 0) // d_)
           == (lax.broadcasted_iota(jnp.int32, (w_, w_), 1) // d_)).astype(F32)
    first_row = lax.broadcasted_iota(jnp.int32, (chunk, 1), 0) == 0

    def bd(x):
        return _stack_heads_masked(x, h_, d_)

    def tn(a, b):
        if a.shape[0] < LANES:
            pad = jnp.zeros((LANES - a.shape[0], a.shape[1]), F32)
            a = jnp.concatenate([a, pad], axis=0)
            b = jnp.concatenate([b, jnp.zeros((pad.shape[0], b.shape[1]), F32)], axis=0)
        return _doth(a.T, b)

    def body(c, carry):
        prev_rkv, prev_wdad = carry
        start = pl.multiple_of(c * chunk, chunk)
        p_rkv = rkv_ref[pl.ds(start, chunk), :]
        p_wdad = wdad_ref[pl.ds(start, chunk), :]
        sh_rkv = jnp.where(first_row, prev_rkv, pltpu.roll(p_rkv, 1, 0))
        sh_wdad = jnp.where(first_row, prev_wdad, pltpu.roll(p_wdad, 1, 0))
        m_rkv = p_rkv + mu_rkv_ref[...] * (sh_rkv - p_rkv)
        m_wdad = p_wdad + mu_wdad_ref[...] * (sh_wdad - p_wdad)
        r, k, v = m_rkv[:, :w_], m_rkv[:, w_:2 * w_], m_rkv[:, 2 * w_:]
        z = -(w0_ref[...] + _dot(jnp.tanh(m_wdad), w2_ref[...]))
        softplus = jnp.maximum(z, 0.0) + jnp.log(1.0 + jnp.exp(-jnp.abs(z)))
        lw = -jnp.exp(-softplus - 0.5)
        a = _sigmoid(a0_ref[...] + _dot(m_wdad, a2_ref[...]))
        kk = k * kk_ref[...]
        kk = kk / jnp.maximum(jnp.sqrt(_doth(kk * kk, seg)), 1e-12)
        k_h = k * (1.0 + (a - 1.0) * ka_ref[...])
        b = kk * a
        bonus = _doth(r * k_h * rk_ref[...], seg) * v

        g = _doth(tri, lw)
        g_end = g[chunk - 1:chunk, :]
        k_til = bd(kk * jnp.exp(g - lw))
        r_til = bd(r * jnp.exp(g))
        inv_g = jnp.exp(-g)
        k_hat = bd(k_h * inv_g)
        b_hat = bd(b * inv_g)
        to_end = jnp.exp(g_end - g)
        k_bar = bd(k_h * to_end)
        b_bar = bd(b * to_end)
        v_bd = bd(v)
        s_bd = s_ref[...]

        a_b = jnp.where(strict, _doth_nt(k_til, b_hat), 0.0)
        a_k = jnp.where(strict, _doth_nt(k_til, k_hat), 0.0)
        a_rk = jnp.where(incl, _doth_nt(r_til, k_hat), 0.0)
        a_rb = jnp.where(incl, _doth_nt(r_til, b_hat), 0.0)
        t_inv = _unit_lower_inverse(a_b, eye, chunk, row_t, col_t)
        u = _doth(t_inv, _doth_nt(k_til, s_bd) + _doth(a_k, v_bd))
        o_bd = _doth_nt(r_til, s_bd) + _doth(a_rk, v_bd) - _doth(a_rb, u)
        s_ref[...] = s_bd * jnp.exp(g_end) + tn(v_bd, k_bar) - tn(u, b_bar)

        out = o_bd[0:chunk]
        for h in range(1, h_):
            out = out + o_bd[h * chunk:(h + 1) * chunk]
        mean = _doth(out, seg) * (1.0 / d_)
        cen = out - mean
        var = _doth(cen * cen, seg) * (1.0 / d_)
        out = cen * lax.rsqrt(var + GN_EPS) * gng_ref[...] + gnb_ref[...]
        o_ref[pl.ds(start, chunk), :] = out + bonus
        return p_rkv[chunk - 1:chunk, :], p_wdad[chunk - 1:chunk, :]

    lax.fori_loop(0, n_chunks, body, (prkv_ref[...], pwdad_ref[...]))


def _rwkv(rkv, wdad, prev_rkv, prev_wdad, s_bd, params):
    b, t, _ = rkv.shape
    chunk = min(RWKV_CHUNK, t)
    n_chunks = t // chunk
    seq = lambda w: pl.BlockSpec((None, t, w), lambda i: (i, 0, 0))
    one = lambda w: pl.BlockSpec((None, 1, w), lambda i: (i, 0, 0))
    full = lambda a: pl.BlockSpec(a.shape, lambda i: (0,) * a.ndim)
    st = pl.BlockSpec((None, C_WIDTH, C_WIDTH), lambda i: (i, 0, 0))
    return pl.pallas_call(
        functools.partial(_rwkv_kernel, chunk=chunk, n_chunks=n_chunks),
        grid=(b,),
        in_specs=[seq(3 * C_WIDTH), seq(LANES), one(3 * C_WIDTH), one(LANES), st]
                 + [full(p) for p in params],
        out_specs=[seq(C_WIDTH), st],
        out_shape=[jax.ShapeDtypeStruct((b, t, C_WIDTH), F32),
                   jax.ShapeDtypeStruct((b, C_WIDTH, C_WIDTH), F32)],
        compiler_params=_cp(("arbitrary",)),
        name="rwkv7",
    )(rkv, wdad, prev_rkv, prev_wdad, s_bd, *params)


def _state_to_bd(state):
    b = state.shape[0]
    z = jnp.zeros((b, C_HEAD_DIM, C_HEAD_DIM), state.dtype)
    rows = [jnp.concatenate([state[:, h] if g == h else z for g in range(C_HEADS)], axis=2)
            for h in range(C_HEADS)]
    return jnp.concatenate(rows, axis=1)


def _state_from_bd(s_bd):
    d = C_HEAD_DIM
    return jnp.stack([s_bd[:, h * d:(h + 1) * d, h * d:(h + 1) * d] for h in range(C_HEADS)], axis=1)


def _out_kernel(x_ref, gate_ref, oa_ref, ga_ref, ob_ref, gb_ref, oc_ref, gc_ref, w_ref, lng_ref,
                lnb_ref, y_ref, *, alpha):
    mixed = (jnp.dot((oa_ref[...] * _silu(ga_ref[...])).astype(BF16), w_ref[0:A_WIDTH, :],
                     preferred_element_type=F32)
             + jnp.dot((ob_ref[...] * _silu(gb_ref[...])).astype(BF16),
                       w_ref[A_WIDTH:A_WIDTH + B_WIDTH, :], preferred_element_type=F32)
             + jnp.dot((oc_ref[...] * _silu(gc_ref[...])).astype(BF16),
                       w_ref[A_WIDTH + B_WIDTH:, :], preferred_element_type=F32))
    z = alpha * x_ref[...] + gate_ref[...] * mixed
    mu = jnp.mean(z, axis=-1, keepdims=True)
    zc = z - mu
    var = jnp.mean(zc * zc, axis=-1, keepdims=True)
    y_ref[...] = zc * lax.rsqrt(var + LN_EPS) * lng_ref[...] + lnb_ref[...]


def _out_proj(x2d, gate, tiles_per_mod, oa, ga, ob, gb, oc, gc, w_out, ln_g, ln_b, alpha, tm):
    r = x2d.shape[0]
    rb = gate.shape[1]
    row = lambda wd: pl.BlockSpec((tm, wd), lambda i: (i, 0))
    full = lambda a: pl.BlockSpec(a.shape, lambda i: (0,) * a.ndim)
    mod = pl.BlockSpec((None, rb, D_MODEL), lambda i: (i // tiles_per_mod, 0, 0))
    return pl.pallas_call(
        functools.partial(_out_kernel, alpha=alpha),
        grid=(r // tm,),
        in_specs=[row(D_MODEL), mod, row(A_WIDTH), row(A_WIDTH), row(B_WIDTH), row(B_WIDTH),
                  row(C_WIDTH), row(C_WIDTH), full(w_out), full(ln_g), full(ln_b)],
        out_specs=row(D_MODEL),
        out_shape=jax.ShapeDtypeStruct((r, D_MODEL), F32),
        compiler_params=_cp(("arbitrary",)),
        name="out_proj",
    )(x2d, gate, oa, ga, ob, gb, oc, gc, w_out, ln_g, ln_b)


def _page_specs(layer, pg, width):
    return [pl.BlockSpec((None, None, PAGE_SIZE, width),
                         lambda b, g, pt, j=j: (layer, pt[b, g * pg + j], 0, 0)) for j in range(pg)]


def _moba_select_kernel(pt_ref, q_ref, *rest, pg, n_blk):
    k_refs, sel_ref, kmean_ref = rest[:pg], rest[pg], rest[pg + 1]
    g = pl.program_id(1)
    per = MOBA_BLOCK // PAGE_SIZE
    sums = [jnp.sum(k_refs[j][...], axis=0, keepdims=True) for j in range(pg)]
    means = [sum(sums[i * per:(i + 1) * per]) * (1.0 / MOBA_BLOCK) for i in range(pg // per)]
    start = pl.multiple_of(g * (pg // per), pg // per)
    kmean_ref[pl.ds(start, pg // per), :] = jnp.concatenate(means, axis=0)

    @pl.when(g == pl.num_programs(1) - 1)
    def _():
        qm = _stack_heads_masked(q_ref[...], A_HEADS, A_HEAD_DIM)
        gate = _doth_nt(qm, kmean_ref[...])
        sel_ref[...] = _top_select(gate, jnp.ones(gate.shape, jnp.bool_), 1).astype(F32)


def _moba_attend_kernel(pt_ref, q_ref, kn_ref, vn_ref, sel_ref, *rest, pg, n_blk):
    k_refs, v_refs = rest[:pg], rest[pg:2 * pg]
    o_ref, m_ref, l_ref, acc_ref = rest[2 * pg:]
    g = pl.program_id(1)
    t = q_ref.shape[0]
    rows = A_HEADS * t
    qb = (_stack_heads_masked(q_ref[...], A_HEADS, A_HEAD_DIM) * (A_HEAD_DIM ** -0.5)).astype(BF16)

    @pl.when(g == 0)
    def _():
        s = _dot_nt(qb, kn_ref[...])
        qt = lax.broadcasted_iota(jnp.int32, (rows, t), 0) % t
        kt = lax.broadcasted_iota(jnp.int32, (rows, t), 1)
        s = jnp.where(kt <= qt, s, NEG)
        m = jnp.max(s, axis=-1, keepdims=True)
        p = jnp.exp(s - m)
        m_ref[...] = m
        l_ref[...] = jnp.sum(p, axis=-1, keepdims=True)
        acc_ref[...] = _dot(p, vn_ref[...])

    kc = jnp.concatenate([r[...].astype(BF16) for r in k_refs], axis=0)
    vc = jnp.concatenate([r[...].astype(BF16) for r in v_refs], axis=0)
    keys = pg * PAGE_SIZE
    s = _dot_nt(qb, kc)
    blk_of_key = g * (keys // MOBA_BLOCK) + lax.broadcasted_iota(jnp.int32, (n_blk, keys), 1) // MOBA_BLOCK
    expand = (lax.broadcasted_iota(jnp.int32, (n_blk, keys), 0) == blk_of_key).astype(BF16)
    allowed = jnp.dot(sel_ref[...].astype(BF16), expand, preferred_element_type=F32) > 0.5
    s = jnp.where(allowed, s, NEG)
    m, l, acc = _softmax_step(s, m_ref[...], l_ref[...], acc_ref[...], vc)
    m_ref[...] = m
    l_ref[...] = l
    acc_ref[...] = acc

    @pl.when(g == pl.num_programs(1) - 1)
    def _():
        o_ref[...] = _merge_heads(acc / l, A_HEADS, A_HEAD_DIM)


def _moba_sample(q, k_new, v_new, pool_k, pool_v, page_table, layer, pg):
    db, t, w = q.shape
    n_pages = page_table.shape[1]
    n_blk = n_pages * PAGE_SIZE // MOBA_BLOCK
    ng = n_pages // pg
    rows = A_HEADS * t
    tok = pl.BlockSpec((None, t, w), lambda b, g, pt: (b, 0, 0))
    selspec = pl.BlockSpec((None, rows, n_blk), lambda b, g, pt: (b, 0, 0))
    sel = pl.pallas_call(
        functools.partial(_moba_select_kernel, pg=pg, n_blk=n_blk),
        grid_spec=pltpu.PrefetchScalarGridSpec(
            num_scalar_prefetch=1, grid=(db, ng),
            in_specs=[tok] + _page_specs(layer, pg, w),
            out_specs=selspec,
            scratch_shapes=[pltpu.VMEM((n_blk, w), F32)]),
        out_shape=jax.ShapeDtypeStruct((db, rows, n_blk), F32),
        compiler_params=_cp(("arbitrary", "arbitrary")),
        name="moba_sample_select",
    )(page_table, q, *([pool_k] * pg))
    return pl.pallas_call(
        functools.partial(_moba_attend_kernel, pg=pg, n_blk=n_blk),
        grid_spec=pltpu.PrefetchScalarGridSpec(
            num_scalar_prefetch=1, grid=(db, ng),
            in_specs=[tok, tok, tok, selspec] + _page_specs(layer, pg, w) + _page_specs(layer, pg, w),
            out_specs=tok,
            scratch_shapes=[pltpu.VMEM((rows, 1), F32), pltpu.VMEM((rows, 1), F32),
                            pltpu.VMEM((rows, w), F32)]),
        out_shape=jax.ShapeDtypeStruct((db, t, w), F32),
        compiler_params=_cp(("arbitrary", "arbitrary")),
        name="moba_sample_attend",
    )(page_table, q, k_new, v_new, sel, *([pool_k] * pg), *([pool_v] * pg))


def _mla_sample_kernel(pt_ref, qabs_ref, qrope_ref, cn_ref, rn_ref, wuv_ref, *rest, pg):
    c_refs, r_refs = rest[:pg], rest[pg:2 * pg]
    o_ref, m_ref, l_ref, acc_ref = rest[2 * pg:]
    g = pl.program_id(1)
    t = qabs_ref.shape[0]
    rows = B_HEADS * t
    qs, qr = _stack_mla_q(qabs_ref, qrope_ref)

    @pl.when(g == 0)
    def _():
        cn = cn_ref[...].astype(BF16)
        s = _dot_nt(qs, cn) + _dot_nt(qr, rn_ref[...])
        qt = lax.broadcasted_iota(jnp.int32, (rows, t), 0) % t
        kt = lax.broadcasted_iota(jnp.int32, (rows, t), 1)
        s = jnp.where(kt <= qt, s, NEG)
        m = jnp.max(s, axis=-1, keepdims=True)
        p = jnp.exp(s - m)
        m_ref[...] = m
        l_ref[...] = jnp.sum(p, axis=-1, keepdims=True)
        acc_ref[...] = _dot(p, cn)

    cc = jnp.concatenate([r[...].astype(BF16) for r in c_refs], axis=0)
    rc = jnp.concatenate([r[...].astype(BF16) for r in r_refs], axis=0)
    s = _dot_nt(qs, cc) + _dot_nt(qr, rc)
    m, l, acc = _softmax_step(s, m_ref[...], l_ref[...], acc_ref[...], cc)
    m_ref[...] = m
    l_ref[...] = l
    acc_ref[...] = acc

    @pl.when(g == pl.num_programs(1) - 1)
    def _():
        _mla_finish(acc, l, wuv_ref, o_ref, t)


def _mla_sample(qabs, qrope, c_new, r_new, pool_c, pool_r, page_table, wuv, layer, pg):
    db, t, _ = qabs.shape
    n_pages = page_table.shape[1]
    rows = B_HEADS * t
    tok = lambda w: pl.BlockSpec((None, t, w), lambda b, g, pt: (b, 0, 0))
    return pl.pallas_call(
        functools.partial(_mla_sample_kernel, pg=pg),
        grid_spec=pltpu.PrefetchScalarGridSpec(
            num_scalar_prefetch=1, grid=(db, n_pages // pg),
            in_specs=[tok(qabs.shape[2]), tok(qrope.shape[2]), tok(B_KV_RANK), tok(B_ROPE_DIM),
                      pl.BlockSpec(wuv.shape, lambda b, g, pt: (0, 0, 0))]
                     + _page_specs(layer, pg, B_KV_RANK) + _page_specs(layer, pg, B_ROPE_DIM),
            out_specs=tok(B_WIDTH),
            scratch_shapes=[pltpu.VMEM((rows, 1), F32), pltpu.VMEM((rows, 1), F32),
                            pltpu.VMEM((rows, B_KV_RANK), F32)]),
        out_shape=jax.ShapeDtypeStruct((db, t, B_WIDTH), F32),
        compiler_params=_cp(("arbitrary", "arbitrary")),
        name="mla_sample",
    )(page_table, qabs, qrope, c_new, r_new, wuv, *([pool_c] * pg), *([pool_r] * pg))


def _prep_w_in(w):
    z = jnp.zeros((w.shape[0], 64), w.dtype)
    return jnp.concatenate(
        [w[:, 0:1536], w[:, 1600:2112], w[:, 2112:2880], w[:, 3008:3264], w[:, 1536:1600], z,
         w[:, 2880:3008]], axis=1).astype(BF16)


def _prep_w_uq(w):
    w = w.reshape(B_Q_RANK, B_HEADS, B_NOPE_DIM + B_ROPE_DIM)
    nope = w[:, :, :B_NOPE_DIM].reshape(B_Q_RANK, B_HEADS * B_NOPE_DIM)
    rope = jnp.pad(w[:, :, B_NOPE_DIM:], ((0, 0), (0, 0), (0, LANES - B_ROPE_DIM)))
    return jnp.concatenate([nope, rope.reshape(B_Q_RANK, B_HEADS * LANES)], axis=1).astype(BF16)


def _pad_rows_64(w, top):
    z = jnp.zeros((64, w.shape[1]), w.dtype)
    return jnp.concatenate([w, z] if top else [z, w], axis=0)


def kernel(x_prompt, x_sample, cache_moba_k, cache_moba_v, cache_mla_latent, cache_mla_rope, state_rwkv_wkv, state_rwkv_shift, page_table, c_prompt, c_sample, w_ada, b_ada, w_in, mla_q_norm, mla_kv_norm, mla_w_uq, mla_w_uk, mla_w_uv, rwkv_mu, rwkv_w0, rwkv_w2, rwkv_a0, rwkv_a2, rwkv_k_k, rwkv_k_a, rwkv_r_k, rwkv_gn_g, rwkv_gn_b, w_out, ln_g, ln_b):
    depth = w_in.shape[0]
    bp, sp, d = x_prompt.shape
    db, ts, _ = x_sample.shape
    n_pages = page_table.shape[1]
    n_pool = cache_moba_k.shape[1]
    past = n_pages * PAGE_SIZE
    alpha = (2 * depth) ** 0.25
    tm_p = min(256, sp)
    rs = db * ts
    tm_s = rs
    pg = min(16, n_pages)

    mods = _ada_all(jnp.concatenate([c_prompt, c_sample], axis=0), w_ada, b_ada)

    pos_p = jnp.arange(sp, dtype=jnp.int32)
    pos_s = jnp.tile(past + jnp.arange(ts, dtype=jnp.int32), db)
    tabs = [(_rope_tables(p, A_ROPE_DIM // 2, ROPE_THETA, A_HEAD_DIM, A_HEADS),
             _rope_tables(p, B_ROPE_DIM // 2, MLA_ROPE_THETA, LANES, 1)) for p in (pos_p, pos_s)]

    pool_k = cache_moba_k.reshape(depth, n_pool, PAGE_SIZE, A_WIDTH)
    pool_v = cache_moba_v.reshape(depth, n_pool, PAGE_SIZE, A_WIDTH)

    xp = x_prompt.reshape(bp * sp, d)
    xs = x_sample.reshape(rs, d)
    zero_state = jnp.zeros((bp, C_WIDTH, C_WIDTH), F32)
    outs = [[] for _ in range(12)]

    for l in range(depth):
        w = _prep_w_in(w_in[l])
        wuq = _prep_w_uq(mla_w_uq[l])
        wuk = jnp.transpose(mla_w_uk[l], (1, 2, 0)).astype(BF16)
        wuv = jnp.transpose(mla_w_uv[l], (1, 0, 2)).astype(BF16)
        qn = mla_q_norm[l].reshape(1, -1)
        kvn = mla_kv_norm[l].reshape(1, -1)
        mu = rwkv_mu[l]
        row = lambda a: a.reshape(1, -1)
        rwkv_params = (row(mu[:3 * C_WIDTH]), row(mu[3 * C_WIDTH:]), row(rwkv_w0[l]),
                       _pad_rows_64(rwkv_w2[l], True).astype(BF16), row(rwkv_a0[l]),
                       _pad_rows_64(rwkv_a2[l], False).astype(BF16), row(rwkv_k_k[l]),
                       row(rwkv_k_a[l]), row(rwkv_r_k[l]), row(rwkv_gn_g[l]), row(rwkv_gn_b[l]))
        wo = w_out[l].astype(BF16)
        lng, lnb = row(ln_g[l]), row(ln_b[l])

        for stream in (0, 1):
            if stream == 0:
                x2d, nb_, t_, tm = xp, bp, sp, tm_p
                mod = mods[l, :bp].reshape(bp, 1, 3 * d)
                tiles_per_mod = sp // tm
                prev = jnp.zeros((bp, 3 * C_WIDTH + LANES), F32)
                s0 = zero_state
            else:
                x2d, nb_, t_, tm = xs, db, ts, tm_s
                mod = jnp.repeat(mods[l, bp:], ts, axis=0).reshape(1, rs, 3 * d)
                tiles_per_mod = 1
                prev = state_rwkv_shift[l]
                s0 = _state_to_bd(state_rwkv_wkv[l])
            shift, scale, gate = mod[..., :d], mod[..., d:2 * d], mod[..., 2 * d:]
            tab_a, tab_b = tabs[stream]
            (qa, ka, va, ga, qabs, qrope, ckv, kr, gb, rkv, wdad, gc) = _in_proj(
                x2d, scale, shift, tiles_per_mod, w, wuq, wuk, qn, kvn, tab_a, tab_b, tm)
            r3 = lambda a: a.reshape(nb_, t_, a.shape[-1])
            if stream == 0:
                oa = _moba_prompt(r3(qa), r3(ka), r3(va))
                ob = _mla_prompt(r3(qabs), r3(qrope), r3(ckv), r3(kr), wuv)
            else:
                oa = _moba_sample(r3(qa), r3(ka), r3(va), pool_k, pool_v, page_table, l, pg)
                ob = _mla_sample(r3(qabs), r3(qrope), r3(ckv), r3(kr), cache_mla_latent,
                                 cache_mla_rope, page_table, wuv, l, pg)
            oc, s_new = _rwkv(r3(rkv), r3(wdad), prev[:, None, :3 * C_WIDTH],
                              prev[:, None, 3 * C_WIDTH:], s0, rwkv_params)
            y = _out_proj(x2d, gate, tiles_per_mod, oa.reshape(-1, A_WIDTH), ga,
                          ob.reshape(-1, B_WIDTH), gb, oc.reshape(-1, C_WIDTH), gc, wo, lng, lnb,
                          alpha, tm)
            last = jnp.concatenate([r3(rkv)[:, -1], r3(wdad)[:, -1]], axis=-1)
            leaves = (ka.reshape(nb_, t_, A_HEADS, A_HEAD_DIM), va.reshape(nb_, t_, A_HEADS, A_HEAD_DIM),
                      r3(ckv), r3(kr), _state_from_bd(s_new), last)
            for i, leaf in enumerate(leaves):
                outs[stream * 6 + i].append(leaf)
            if stream == 0:
                xp = y
            else:
                xs = y

    return (xp.reshape(bp, sp, d), xs.reshape(db, ts, d)) + tuple(jnp.stack(o) for o in outs)
```

```python
import functools
import math

import jax
import jax.numpy as jnp
from jax import lax
from jax.experimental import pallas as pl
from jax.experimental.pallas import tpu as pltpu

F32 = jnp.float32
BF16 = jnp.bfloat16
HI = lax.Precision.HIGHEST

D_MODEL = 1024
PAGE_SIZE = 128
LANES = 128
A_HEADS = 4
A_HEAD_DIM = 64
A_WIDTH = 256
A_ROPE_DIM = 16
ROPE_THETA = 500000.0
MOBA_BLOCK = 256
MOBA_TOPK = 3
Q_BLOCK = 128
B_HEADS = 4
B_NOPE_DIM = 128
B_ROPE_DIM = 64
B_V_DIM = 128
B_WIDTH = 512
B_Q_RANK = 256
B_KV_RANK = 256
MLA_ROPE_THETA = 10000.0
MLA_SCALE = (B_NOPE_DIM + B_ROPE_DIM) ** -0.5
C_HEADS = 4
C_HEAD_DIM = 64
C_WIDTH = 256
C_DECAY_RANK = 64
C_A_RANK = 64
GN_EPS = 64e-5
RWKV_CHUNK = 64
RWKV_SUB = 16
LN_EPS = 1e-5
RMS_EPS = 1e-6
NEG = -1e30

OFF_QA, OFF_KA, OFF_VA, OFF_GA = 0, 256, 512, 768
OFF_CQ, OFF_CKV, OFF_GB, OFF_RKV, OFF_GC, OFF_KR, OFF_WDAD = 1024, 1280, 1536, 2048, 2816, 3072, 3200
IN_COLS_PAD = 3328

VMEM_LIMIT = 56 * 1024 * 1024


def _cp(sem):
    return pltpu.CompilerParams(dimension_semantics=sem, vmem_limit_bytes=VMEM_LIMIT)


def _dot(a, b):
    return jnp.dot(a.astype(BF16), b.astype(BF16), preferred_element_type=F32)


def _dot_nt(a, b):
    return lax.dot_general(a.astype(BF16), b.astype(BF16), (((1,), (1,)), ((), ())),
                           preferred_element_type=F32)


def _doth(a, b):
    return jnp.dot(a, b, precision=HI, preferred_element_type=F32)


def _doth_nt(a, b):
    return lax.dot_general(a, b, (((1,), (1,)), ((), ())), precision=HI, preferred_element_type=F32)


_NN = (((1,), (0,)), ((), ()))
_NT = (((1,), (1,)), ((), ()))


def _split(x):
    hi = x.astype(BF16)
    return hi, (x - hi.astype(F32)).astype(BF16)


def _bdot(a, b, dims=_NN):
    return lax.dot_general(a.astype(BF16), b.astype(BF16), dims, preferred_element_type=F32)


def _sigmoid(x):
    return 1.0 / (1.0 + jnp.exp(-x))


def _silu(x):
    return x * _sigmoid(x)


def _head_mask(rows, width, head, head_dim):
    lane = lax.broadcasted_iota(jnp.int32, (rows, width), 1)
    return (lane >= head * head_dim) & (lane < (head + 1) * head_dim)


def _ada_kernel(c_ref, w_ref, b_ref, o_ref):
    o_ref[...] = _dot(_silu(c_ref[...]), w_ref[...]) + b_ref[...]


def _ada_all(c_all, w_ada, b_ada):
    depth = w_ada.shape[0]
    r = c_all.shape[0]
    nj = w_ada.shape[2] // D_MODEL
    return pl.pallas_call(
        _ada_kernel,
        grid=(depth, nj),
        in_specs=[pl.BlockSpec((r, D_MODEL), lambda l, j: (0, 0)),
                  pl.BlockSpec((None, D_MODEL, D_MODEL), lambda l, j: (l, 0, j)),
                  pl.BlockSpec((None, 1, D_MODEL), lambda l, j: (l, 0, j))],
        out_specs=pl.BlockSpec((None, r, D_MODEL), lambda l, j: (l, 0, j)),
        out_shape=jax.ShapeDtypeStruct((depth, r, w_ada.shape[2]), F32),
        compiler_params=_cp(("arbitrary", "arbitrary")),
        name="adaln_mod",
    )(c_all, w_ada, b_ada.reshape(depth, 1, -1))


def _rope_tables(pos, rot_half, theta, head_width, n_rep):
    inv = jnp.power(theta, -jnp.arange(rot_half, dtype=F32) / rot_half)
    ang = pos.astype(F32)[:, None] * inv[None, :]
    cos, sin = jnp.cos(ang), jnp.sin(ang)
    p = pos.shape[0]
    rest = head_width - 2 * rot_half
    c = jnp.concatenate([cos, cos, jnp.ones((p, rest), F32)], axis=1)
    s1 = jnp.concatenate([-sin, jnp.zeros((p, head_width - rot_half), F32)], axis=1)
    s2 = jnp.concatenate([jnp.zeros((p, rot_half), F32), sin, jnp.zeros((p, rest), F32)], axis=1)
    return jnp.stack([jnp.tile(t, (1, n_rep)) for t in (c, s1, s2)])


def _apply_rope(x, cos, s1, s2, half):
    w = x.shape[-1]
    return x * cos + pltpu.roll(x, w - half, 1) * s1 + pltpu.roll(x, half, 1) * s2


def _rms(x, g):
    return x * lax.rsqrt(jnp.mean(x * x, axis=-1, keepdims=True) + RMS_EPS) * g


def _in_kernel(x_ref, sc_ref, sh_ref, w_ref, wuq_ref, wuk_ref, qn_ref, kvn_ref, ta_ref, tb_ref,
               qa_ref, ka_ref, va_ref, ga_ref, qabs_ref, qrope_ref, ckv_ref, kr_ref, gb_ref,
               rkv_ref, wdad_ref, gc_ref):
    hb = (x_ref[...] * (1.0 + sc_ref[...]) + sh_ref[...]).astype(BF16)

    def proj(off, width):
        return jnp.dot(hb, w_ref[:, off:off + width], preferred_element_type=F32)

    ca, sa1, sa2 = ta_ref[0], ta_ref[1], ta_ref[2]
    cb, sb1, sb2 = tb_ref[0], tb_ref[1], tb_ref[2]
    qa_ref[...] = _apply_rope(proj(OFF_QA, A_WIDTH), ca, sa1, sa2, A_ROPE_DIM // 2)
    ka_ref[...] = _apply_rope(proj(OFF_KA, A_WIDTH), ca, sa1, sa2, A_ROPE_DIM // 2)
    va_ref[...] = proj(OFF_VA, A_WIDTH)
    ga_ref[...] = proj(OFF_GA, A_WIDTH)
    cqn = _rms(proj(OFF_CQ, B_Q_RANK), qn_ref[...])
    qf = _dot(cqn, wuq_ref[...])
    for h in range(B_HEADS):
        qabs_ref[:, h * B_KV_RANK:(h + 1) * B_KV_RANK] = _dot(
            qf[:, h * B_NOPE_DIM:(h + 1) * B_NOPE_DIM], wuk_ref[h])
        base = B_HEADS * B_NOPE_DIM + h * LANES
        qrope_ref[:, h * LANES:(h + 1) * LANES] = _apply_rope(
            qf[:, base:base + LANES], cb, sb1, sb2, B_ROPE_DIM // 2)
    ckv_ref[...] = _rms(proj(OFF_CKV, B_KV_RANK), kvn_ref[...])
    kr_ref[...] = _apply_rope(proj(OFF_KR, LANES), cb, sb1, sb2, B_ROPE_DIM // 2)[:, :B_ROPE_DIM]
    gb_ref[...] = proj(OFF_GB, B_WIDTH)
    rkv_ref[...] = proj(OFF_RKV, 3 * C_WIDTH)
    wdad_ref[...] = proj(OFF_WDAD, LANES)
    gc_ref[...] = proj(OFF_GC, C_WIDTH)


def _in_proj(x2d, scale, shift, tiles_per_mod, w, wuq, wuk, qn, kvn, tab_a, tab_b, tm):
    r = x2d.shape[0]
    n_tiles = r // tm
    tab_tiles = tab_a.shape[1] // tm
    rb = scale.shape[1]
    widths = [A_WIDTH, A_WIDTH, A_WIDTH, A_WIDTH, B_HEADS * B_KV_RANK, B_HEADS * LANES, B_KV_RANK,
              B_ROPE_DIM, B_WIDTH, 3 * C_WIDTH, LANES, C_WIDTH]
    row = lambda wd: pl.BlockSpec((tm, wd), lambda i: (i, 0))
    full = lambda a: pl.BlockSpec(a.shape, lambda i: (0,) * a.ndim)
    mod = pl.BlockSpec((None, rb, D_MODEL), lambda i: (i // tiles_per_mod, 0, 0))
    tab = lambda a: pl.BlockSpec((3, tm, a.shape[2]), lambda i: (0, i % tab_tiles, 0))
    return pl.pallas_call(
        _in_kernel,
        grid=(n_tiles,),
        in_specs=[row(D_MODEL), mod, mod, full(w), full(wuq), full(wuk), full(qn), full(kvn),
                  tab(tab_a), tab(tab_b)],
        out_specs=[row(wd) for wd in widths],
        out_shape=[jax.ShapeDtypeStruct((r, wd), F32) for wd in widths],
        compiler_params=_cp(("arbitrary",)),
        name="in_proj",
    )(x2d, scale, shift, w, wuq, wuk, qn, kvn, tab_a, tab_b)


def _top_select(g, valid, axis):
    n = g.shape[axis]
    idx = lax.broadcasted_iota(jnp.int32, g.shape, axis).astype(F32)
    g = jnp.where(valid, g, -jnp.inf)
    sel = jnp.zeros(g.shape, jnp.bool_)
    for _ in range(min(MOBA_TOPK, n)):
        mx = jnp.max(g, axis=axis, keepdims=True)
        first = jnp.min(jnp.where(g == mx, idx, float(n)), axis=axis, keepdims=True)
        hit = idx == first
        sel = sel | hit
        g = jnp.where(hit, -jnp.inf, g)
    return sel & valid


def _stack_heads_masked(q, n_heads, head_dim):
    r, w = q.shape
    return jnp.concatenate(
        [jnp.where(_head_mask(r, w, h, head_dim), q, 0.0) for h in range(n_heads)], axis=0)


def _merge_heads(acc, n_heads, head_dim):
    r = acc.shape[0] // n_heads
    w = acc.shape[1]
    out = jnp.zeros((r, w), F32)
    for h in range(n_heads):
        out = out + jnp.where(_head_mask(r, w, h, head_dim), acc[h * r:(h + 1) * r], 0.0)
    return out


def _softmax_reset(m_ref, l_ref, acc_ref):
    m_ref[...] = jnp.full(m_ref.shape, NEG, F32)
    l_ref[...] = jnp.zeros(l_ref.shape, F32)
    acc_ref[...] = jnp.zeros(acc_ref.shape, F32)


def _softmax_update(s, m_ref, l_ref, acc_ref, v, group, nt=False):
    ps = []
    for g0 in range(0, s.shape[0], group):
        sl = slice(g0, g0 + group)
        sg = s[sl]
        m_old = m_ref[sl]
        m_new = jnp.maximum(m_old, jnp.max(sg, axis=-1, keepdims=True))
        alpha = jnp.exp(m_old - m_new)
        p = jnp.exp(sg - m_new)
        l_ref[sl] = alpha * l_ref[sl] + jnp.sum(p, axis=-1, keepdims=True)
        m_ref[sl] = m_new
        acc_ref[sl] = alpha * acc_ref[sl]
        ps.append(p.astype(BF16))
    p_all = ps[0] if len(ps) == 1 else jnp.concatenate(ps, axis=0)
    acc_ref[...] += lax.dot_general(p_all, v, _NT if nt else _NN, preferred_element_type=F32)


def _softmax_step(s, m, l, acc, v):
    m_new = jnp.maximum(m, jnp.max(s, axis=-1, keepdims=True))
    alpha = jnp.exp(m - m_new)
    p = jnp.exp(s - m_new)
    l = alpha * l + jnp.sum(p, axis=-1, keepdims=True)
    acc = alpha * acc + _dot(p, v)
    return m_new, l, acc


def _moba_prompt_kernel(q_ref, k_ref, v_ref, o_ref, kmean_ref, *, nb):
    c = pl.program_id(1)
    own = (c * Q_BLOCK) // MOBA_BLOCK

    @pl.when(c == 0)
    def _():
        kmean_ref[...] = jnp.concatenate(
            [jnp.mean(k_ref[n * MOBA_BLOCK:(n + 1) * MOBA_BLOCK, :], axis=0, keepdims=True)
             for n in range(nb)], axis=0)

    scale = A_HEAD_DIM ** -0.5
    qm = _stack_heads_masked(q_ref[...], A_HEADS, A_HEAD_DIM)
    rows = A_HEADS * Q_BLOCK
    gate_t = _doth_nt(kmean_ref[...], qm)
    blk = lax.broadcasted_iota(jnp.int32, gate_t.shape, 0)
    sel_t = _top_select(gate_t, blk < own, 0).astype(F32)
    eye = (lax.broadcasted_iota(jnp.int32, (rows, rows), 0)
           == lax.broadcasted_iota(jnp.int32, (rows, rows), 1)).astype(F32)
    sel = _dot_nt(eye, sel_t)

    qpos = c * Q_BLOCK + lax.broadcasted_iota(jnp.int32, (rows, 1), 0) % Q_BLOCK
    qb = (qm * scale).astype(BF16)
    m = jnp.full((rows, 1), NEG, F32)
    l = jnp.zeros((rows, 1), F32)
    acc = jnp.zeros((rows, A_WIDTH), F32)
    for n in range(nb):
        kb = k_ref[n * MOBA_BLOCK:(n + 1) * MOBA_BLOCK, :]
        vb = v_ref[n * MOBA_BLOCK:(n + 1) * MOBA_BLOCK, :]
        s = _dot_nt(qb, kb)
        kpos = n * MOBA_BLOCK + lax.broadcasted_iota(jnp.int32, (1, MOBA_BLOCK), 1)
        allowed = (sel[:, n:n + 1] > 0.5) | ((qpos // MOBA_BLOCK == n) & (kpos <= qpos))
        s = jnp.where(allowed, s, NEG)
        m, l, acc = _softmax_step(s, m, l, acc, vb)
    o_ref[...] = _merge_heads(acc / l, A_HEADS, A_HEAD_DIM)


def _moba_prompt(q, k, v):
    b, s, w = q.shape
    nb = s // MOBA_BLOCK
    blk = pl.BlockSpec((None, Q_BLOCK, w), lambda i, c: (i, c, 0))
    seq = pl.BlockSpec((None, s, w), lambda i, c: (i, 0, 0))
    return pl.pallas_call(
        functools.partial(_moba_prompt_kernel, nb=nb),
        grid=(b, s // Q_BLOCK),
        in_specs=[blk, seq, seq],
        out_specs=blk,
        out_shape=jax.ShapeDtypeStruct((b, s, w), F32),
        scratch_shapes=[pltpu.VMEM((nb, w), F32)],
        compiler_params=_cp(("arbitrary", "arbitrary")),
        name="moba_prompt",
    )(q, k, v)


def _stack_mla_q(qabs_ref, qrope_ref):
    qs = jnp.concatenate([qabs_ref[:, h * B_KV_RANK:(h + 1) * B_KV_RANK] for h in range(B_HEADS)], axis=0)
    qr = jnp.concatenate([qrope_ref[:, h * LANES:h * LANES + B_ROPE_DIM] for h in range(B_HEADS)], axis=0)
    return (qs * MLA_SCALE).astype(BF16), (qr * MLA_SCALE).astype(BF16)


def _mla_finish(acc, l, wuv_ref, o_ref, r):
    o = acc / l
    for h in range(B_HEADS):
        o_ref[:, h * B_V_DIM:(h + 1) * B_V_DIM] = _dot(o[h * r:(h + 1) * r], wuv_ref[h])


def _mla_prompt_kernel(qabs_ref, qrope_ref, ckv_ref, kr_ref, wuv_ref, o_ref, *, tk):
    c = pl.program_id(1)
    rows = B_HEADS * Q_BLOCK
    qs, qr = _stack_mla_q(qabs_ref, qrope_ref)
    qpos = c * Q_BLOCK + lax.broadcasted_iota(jnp.int32, (rows, 1), 0) % Q_BLOCK
    n_chunks = (c * Q_BLOCK + Q_BLOCK + tk - 1) // tk

    def body(j, carry):
        m, l, acc = carry
        start = pl.multiple_of(j * tk, tk)
        kc = ckv_ref[pl.ds(start, tk), :].astype(BF16)
        rc = kr_ref[pl.ds(start, tk), :].astype(BF16)
        s = _dot_nt(qs, kc) + _dot_nt(qr, rc)
        kpos = j * tk + lax.broadcasted_iota(jnp.int32, (1, tk), 1)
        s = jnp.where(kpos <= qpos, s, NEG)
        return _softmax_step(s, m, l, acc, kc)

    m0 = jnp.full((rows, 1), NEG, F32)
    l0 = jnp.zeros((rows, 1), F32)
    a0 = jnp.zeros((rows, B_KV_RANK), F32)
    m, l, acc = lax.fori_loop(0, n_chunks, body, (m0, l0, a0))
    _mla_finish(acc, l, wuv_ref, o_ref, Q_BLOCK)


def _mla_prompt(qabs, qrope, ckv, kr, wuv):
    b, s, _ = qabs.shape
    tk = min(256, s)
    blk = lambda w: pl.BlockSpec((None, Q_BLOCK, w), lambda i, c: (i, c, 0))
    seq = lambda w: pl.BlockSpec((None, s, w), lambda i, c: (i, 0, 0))
    return pl.pallas_call(
        functools.partial(_mla_prompt_kernel, tk=tk),
        grid=(b, s // Q_BLOCK),
        in_specs=[blk(qabs.shape[2]), blk(qrope.shape[2]), seq(B_KV_RANK), seq(B_ROPE_DIM),
                  pl.BlockSpec(wuv.shape, lambda i, c: (0, 0, 0))],
        out_specs=blk(B_WIDTH),
        out_shape=jax.ShapeDtypeStruct((b, s, B_WIDTH), F32),
        compiler_params=_cp(("arbitrary", "arbitrary")),
        name="mla_prompt",
    )(qabs, qrope, ckv, kr, wuv)


def _neumann(a, eye, levels):
    t = eye - a
    p = a.astype(BF16)
    for _ in range(levels - 1):
        p = _bdot(p, p).astype(BF16)
        t = t + _bdot(t, p)
    return t


def _unit_lower_inverse(a, eye, chunk, row_t, col_t):
    if chunk <= RWKV_SUB:
        return _neumann(a, eye, max(1, math.ceil(math.log2(chunk))))
    near = (row_t // RWKV_SUB) == (col_t // RWKV_SUB)
    a_d = jnp.where(near, a, 0.0)
    t_d = _neumann(a_d, eye, int(math.log2(RWKV_SUB))).astype(BF16)
    n = _bdot(t_d, a - a_d)
    t_n = _neumann(n, eye, max(1, math.ceil(math.log2(chunk // RWKV_SUB))))
    return _bdot(t_n, t_d)


def _rwkv_kernel(rkv_ref, wdad_ref, prkv_ref, pwdad_ref, s0_ref, mu_rkv_ref, mu_wdad_ref,
                 w0_ref, w2_ref, a0_ref, a2_ref, kk_ref, ka_ref, rk_ref, gng_ref, gnb_ref,
                 o_ref, s_ref, *, chunk, n_chunks):
    h_, d_, w_ = C_HEADS, C_HEAD_DIM, C_WIDTH
    n = h_ * chunk
    s_ref[...] = s0_ref[...]

    ri = lax.broadcasted_iota(jnp.int32, (n, n), 0)
    ci = lax.broadcasted_iota(jnp.int32, (n, n), 1)
    same = (ri // chunk) == (ci // chunk)
    row_t, col_t = ri % chunk, ci % chunk
    strict = (same & (row_t > col_t))
    incl = (same & (row_t >= col_t))
    eye = (ri == ci).astype(F32)
    tri = (lax.broadcasted_iota(jnp.int32, (chunk, chunk), 0)
           >= lax.broadcasted_iota(jnp.int32, (chunk, chunk), 1)).astype(BF16)
    seg = ((lax.broadcasted_iota(jnp.int32, (w_, w_), 0) // d_)
           == (lax.broadcasted_iota(jnp.int32, (w_, w_), 1) // d_)).astype(BF16)
    first_row = lax.broadcasted_iota(jnp.int32, (chunk, 1), 0) == 0

    def bd(x):
        return _stack_heads_masked(x, h_, d_).astype(BF16)

    def head_sum(x):
        hi, lo = _split(x)
        return (jnp.dot(hi, seg, preferred_element_type=F32)
                + jnp.dot(lo, seg, preferred_element_type=F32))

    def cumsum_rows(x):
        hi = x.astype(BF16)
        r1 = x - hi.astype(F32)
        mid = r1.astype(BF16)
        lo = (r1 - mid.astype(F32)).astype(BF16)
        return (jnp.dot(tri, hi, preferred_element_type=F32)
                + (jnp.dot(tri, mid, preferred_element_type=F32)
                   + jnp.dot(tri, lo, preferred_element_type=F32)))

    def tn(a, b):
        if a.shape[0] < LANES:
            pad = LANES - a.shape[0]
            a = jnp.concatenate([a, jnp.zeros((pad, a.shape[1]), F32)], axis=0)
            b = jnp.concatenate([b, jnp.zeros((pad, b.shape[1]), BF16)], axis=0)
        return _bdot(a.T, b)

    def body(c, carry):
        prev_rkv, prev_wdad = carry
        start = pl.multiple_of(c * chunk, chunk)
        p_rkv = rkv_ref[pl.ds(start, chunk), :]
        p_wdad = wdad_ref[pl.ds(start, chunk), :]
        sh_rkv = jnp.where(first_row, prev_rkv, pltpu.roll(p_rkv, 1, 0))
        sh_wdad = jnp.where(first_row, prev_wdad, pltpu.roll(p_wdad, 1, 0))
        m_rkv = p_rkv + mu_rkv_ref[...] * (sh_rkv - p_rkv)
        m_wdad = p_wdad + mu_wdad_ref[...] * (sh_wdad - p_wdad)
        r, k, v = m_rkv[:, :w_], m_rkv[:, w_:2 * w_], m_rkv[:, 2 * w_:]
        z = -(w0_ref[...] + _dot(jnp.tanh(m_wdad), w2_ref[...]))
        softplus = jnp.maximum(z, 0.0) + jnp.log(1.0 + jnp.exp(-jnp.abs(z)))
        lw = -jnp.exp(-softplus - 0.5)
        a = _sigmoid(a0_ref[...] + _dot(m_wdad, a2_ref[...]))
        kk = k * kk_ref[...]
        kk = kk / jnp.maximum(jnp.sqrt(head_sum(kk * kk)), 1e-12)
        k_h = k * (1.0 + (a - 1.0) * ka_ref[...])
        b = kk * a
        bonus = head_sum(r * k_h * rk_ref[...]) * v

        g = cumsum_rows(lw)
        g_end = g[chunk - 1:chunk, :]
        k_til = bd(kk * jnp.exp(g - lw))
        r_til = bd(r * jnp.exp(g))
        inv_g = jnp.exp(-g)
        k_hat = bd(k_h * inv_g)
        b_hat = bd(b * inv_g)
        to_end = jnp.exp(g_end - g)
        k_bar = bd(k_h * to_end)
        b_bar = bd(b * to_end)
        v_f32 = _stack_heads_masked(v, h_, d_)
        v_bd = v_f32.astype(BF16)
        s_f32 = s_ref[...]
        s_bd = s_f32.astype(BF16)

        a_b = jnp.where(strict, _bdot(k_til, b_hat, _NT), 0.0)
        a_k = jnp.where(strict, _bdot(k_til, k_hat, _NT), 0.0)
        a_rk = jnp.where(incl, _bdot(r_til, k_hat, _NT), 0.0)
        a_rb = jnp.where(incl, _bdot(r_til, b_hat, _NT), 0.0)
        t_inv = _unit_lower_inverse(a_b, eye, chunk, row_t, col_t)
        u_f32 = _bdot(t_inv, _bdot(k_til, s_bd, _NT) + _bdot(a_k, v_bd))
        u = u_f32.astype(BF16)
        o_bd = _bdot(r_til, s_bd, _NT) + _bdot(a_rk, v_bd) - _bdot(a_rb, u)
        s_ref[...] = s_f32 * jnp.exp(g_end) + tn(v_f32, k_bar) - tn(u_f32, b_bar)

        out = o_bd[0:chunk]
        for h in range(1, h_):
            out = out + o_bd[h * chunk:(h + 1) * chunk]
        mean = head_sum(out) * (1.0 / d_)
        cen = out - mean
        var = head_sum(cen * cen) * (1.0 / d_)
        out = cen * lax.rsqrt(var + GN_EPS) * gng_ref[...] + gnb_ref[...]
        o_ref[pl.ds(start, chunk), :] = out + bonus
        return p_rkv[chunk - 1:chunk, :], p_wdad[chunk - 1:chunk, :]

    lax.fori_loop(0, n_chunks, body, (prkv_ref[...], pwdad_ref[...]))


def _rwkv(rkv, wdad, prev_rkv, prev_wdad, s_bd, params):
    b, t, _ = rkv.shape
    chunk = min(RWKV_CHUNK, t)
    n_chunks = t // chunk
    seq = lambda w: pl.BlockSpec((None, t, w), lambda i: (i, 0, 0))
    one = lambda w: pl.BlockSpec((None, 1, w), lambda i: (i, 0, 0))
    full = lambda a: pl.BlockSpec(a.shape, lambda i: (0,) * a.ndim)
    st = pl.BlockSpec((None, C_WIDTH, C_WIDTH), lambda i: (i, 0, 0))
    return pl.pallas_call(
        functools.partial(_rwkv_kernel, chunk=chunk, n_chunks=n_chunks),
        grid=(b,),
        in_specs=[seq(3 * C_WIDTH), seq(LANES), one(3 * C_WIDTH), one(LANES), st]
                 + [full(p) for p in params],
        out_specs=[seq(C_WIDTH), st],
        out_shape=[jax.ShapeDtypeStruct((b, t, C_WIDTH), F32),
                   jax.ShapeDtypeStruct((b, C_WIDTH, C_WIDTH), F32)],
        compiler_params=_cp(("arbitrary",)),
        name="rwkv7",
    )(rkv, wdad, prev_rkv, prev_wdad, s_bd, *params)


def _state_to_bd(state):
    b = state.shape[0]
    z = jnp.zeros((b, C_HEAD_DIM, C_HEAD_DIM), state.dtype)
    rows = [jnp.concatenate([state[:, h] if g == h else z for g in range(C_HEADS)], axis=2)
            for h in range(C_HEADS)]
    return jnp.concatenate(rows, axis=1)


def _state_from_bd(s_bd):
    d = C_HEAD_DIM
    return jnp.stack([s_bd[:, h * d:(h + 1) * d, h * d:(h + 1) * d] for h in range(C_HEADS)], axis=1)


def _out_kernel(x_ref, gate_ref, oa_ref, ga_ref, ob_ref, gb_ref, oc_ref, gc_ref, w_ref, lng_ref,
                lnb_ref, y_ref, *, alpha):
    mixed = (jnp.dot((oa_ref[...] * _silu(ga_ref[...])).astype(BF16), w_ref[0:A_WIDTH, :],
                     preferred_element_type=F32)
             + jnp.dot((ob_ref[...] * _silu(gb_ref[...])).astype(BF16),
                       w_ref[A_WIDTH:A_WIDTH + B_WIDTH, :], preferred_element_type=F32)
             + jnp.dot((oc_ref[...] * _silu(gc_ref[...])).astype(BF16),
                       w_ref[A_WIDTH + B_WIDTH:, :], preferred_element_type=F32))
    z = alpha * x_ref[...] + gate_ref[...] * mixed
    mu = jnp.mean(z, axis=-1, keepdims=True)
    zc = z - mu
    var = jnp.mean(zc * zc, axis=-1, keepdims=True)
    y_ref[...] = zc * lax.rsqrt(var + LN_EPS) * lng_ref[...] + lnb_ref[...]


def _out_proj(x2d, gate, tiles_per_mod, oa, ga, ob, gb, oc, gc, w_out, ln_g, ln_b, alpha, tm):
    r = x2d.shape[0]
    rb = gate.shape[1]
    row = lambda wd: pl.BlockSpec((tm, wd), lambda i: (i, 0))
    full = lambda a: pl.BlockSpec(a.shape, lambda i: (0,) * a.ndim)
    mod = pl.BlockSpec((None, rb, D_MODEL), lambda i: (i // tiles_per_mod, 0, 0))
    return pl.pallas_call(
        functools.partial(_out_kernel, alpha=alpha),
        grid=(r // tm,),
        in_specs=[row(D_MODEL), mod, row(A_WIDTH), row(A_WIDTH), row(B_WIDTH), row(B_WIDTH),
                  row(C_WIDTH), row(C_WIDTH), full(w_out), full(ln_g), full(ln_b)],
        out_specs=row(D_MODEL),
        out_shape=jax.ShapeDtypeStruct((r, D_MODEL), F32),
        compiler_params=_cp(("arbitrary",)),
        name="out_proj",
    )(x2d, gate, oa, ga, ob, gb, oc, gc, w_out, ln_g, ln_b)


def _page_specs(layer, pg, rows, cols):
    return [pl.BlockSpec((None, None, rows, cols),
                         lambda b, g, pt, j=j: (layer, pt[b, g * pg + j], 0, 0)) for j in range(pg)]


def _moba_select_kernel(pt_ref, q_ref, *rest, pg, n_blk):
    k_refs, bias_ref, kmean_ref = rest[:pg], rest[pg], rest[pg + 1]
    g = pl.program_id(1)
    ng = pl.num_programs(1)
    per = MOBA_BLOCK // PAGE_SIZE
    bpg = pg // per

    @pl.when(g == 0)
    def _():
        kmean_ref[...] = jnp.zeros(kmean_ref.shape, F32)

    lane = lax.broadcasted_iota(jnp.int32, kmean_ref.shape, 1)
    upd = jnp.zeros(kmean_ref.shape, F32)
    for i in range(bpg):
        blk_sum = k_refs[i * per][...]
        for j in range(1, per):
            blk_sum = blk_sum + k_refs[i * per + j][...]
        col = jnp.sum(blk_sum, axis=1, keepdims=True) * (1.0 / MOBA_BLOCK)
        upd = jnp.where(lane == g * bpg + i, col, upd)
    kmean_ref[...] += upd

    @pl.when(g == ng - 1)
    def _():
        qm = _stack_heads_masked(q_ref[...], A_HEADS, A_HEAD_DIM)
        gate = _doth(qm, kmean_ref[...])
        blk = lax.broadcasted_iota(jnp.int32, gate.shape, 1)
        bias = jnp.where(_top_select(gate, blk < n_blk, 1), 0.0, NEG)
        r_i = lax.broadcasted_iota(jnp.int32, (LANES, LANES), 0)
        c_i = lax.broadcasted_iota(jnp.int32, (LANES, LANES), 1)
        for gg in range(bias_ref.shape[0]):
            pick = ((r_i == gg * bpg + c_i) & (c_i < bpg)).astype(BF16)
            bias_ref[gg] = _dot(bias, pick)


def _moba_attend_kernel(pt_ref, q_ref, kn_ref, vn_ref, bias_ref, expand_ref, *rest, pg):
    k_refs, v_refs = rest[:pg], rest[pg:2 * pg]
    o_ref, m_ref, l_ref, acc_ref = rest[2 * pg:]
    g = pl.program_id(1)
    t = q_ref.shape[0]
    rows = A_HEADS * t
    bpg = expand_ref.shape[0]
    qb = (_stack_heads_masked(q_ref[...], A_HEADS, A_HEAD_DIM) * (A_HEAD_DIM ** -0.5)).astype(BF16)

    @pl.when(g == 0)
    def _():
        _softmax_reset(m_ref, l_ref, acc_ref)
        s = _dot_nt(qb, kn_ref[...])
        qt = lax.broadcasted_iota(jnp.int32, (rows, t), 0) % t
        kt = lax.broadcasted_iota(jnp.int32, (rows, t), 1)
        _softmax_update(jnp.where(kt <= qt, s, NEG), m_ref, l_ref, acc_ref,
                        vn_ref[...].astype(BF16), rows)

    kc = jnp.concatenate([r[...].astype(BF16) for r in k_refs], axis=1)
    vc = jnp.concatenate([r[...].astype(BF16) for r in v_refs], axis=1)
    s = (jnp.dot(qb, kc, preferred_element_type=F32)
         + jnp.dot(bias_ref[:, :bpg].astype(BF16), expand_ref[...], preferred_element_type=F32))
    _softmax_update(s, m_ref, l_ref, acc_ref, vc, rows, nt=True)

    @pl.when(g == pl.num_programs(1) - 1)
    def _():
        o_ref[...] = _merge_heads(acc_ref[...] / l_ref[...], A_HEADS, A_HEAD_DIM)


def _moba_sample(q, k_new, v_new, pool_kt, pool_vt, page_table, layer, pg):
    db, t, w = q.shape
    n_pages = page_table.shape[1]
    n_blk = n_pages * PAGE_SIZE // MOBA_BLOCK
    assert n_blk <= LANES
    ng = n_pages // pg
    bpg = pg * PAGE_SIZE // MOBA_BLOCK
    rows = A_HEADS * t
    tok = pl.BlockSpec((None, t, w), lambda b, g, pt: (b, 0, 0))
    pages = lambda: _page_specs(layer, pg, w, PAGE_SIZE)
    bias = pl.pallas_call(
        functools.partial(_moba_select_kernel, pg=pg, n_blk=n_blk),
        grid_spec=pltpu.PrefetchScalarGridSpec(
            num_scalar_prefetch=1, grid=(db, ng),
            in_specs=[tok] + pages(),
            out_specs=pl.BlockSpec((None, ng, rows, LANES), lambda b, g, pt: (b, 0, 0, 0)),
            scratch_shapes=[pltpu.VMEM((w, LANES), F32)]),
        out_shape=jax.ShapeDtypeStruct((db, ng, rows, LANES), F32),
        compiler_params=_cp(("arbitrary", "arbitrary")),
        name="moba_sample_select",
    )(page_table, q, *([pool_kt] * pg))
    key_blk = jnp.arange(pg * PAGE_SIZE, dtype=jnp.int32) // MOBA_BLOCK
    expand = (jnp.arange(bpg, dtype=jnp.int32)[:, None] == key_blk[None, :]).astype(BF16)
    return pl.pallas_call(
        functools.partial(_moba_attend_kernel, pg=pg),
        grid_spec=pltpu.PrefetchScalarGridSpec(
            num_scalar_prefetch=1, grid=(db, ng),
            in_specs=[tok, tok, tok,
                      pl.BlockSpec((None, None, rows, LANES), lambda b, g, pt: (b, g, 0, 0)),
                      pl.BlockSpec(expand.shape, lambda b, g, pt: (0, 0))] + pages() + pages(),
            out_specs=tok,
            scratch_shapes=[pltpu.VMEM((rows, 1), F32), pltpu.VMEM((rows, 1), F32),
                            pltpu.VMEM((rows, w), F32)]),
        out_shape=jax.ShapeDtypeStruct((db, t, w), F32),
        compiler_params=_cp(("arbitrary", "arbitrary")),
        name="moba_sample_attend",
    )(page_table, q, k_new, v_new, bias, expand, *([pool_kt] * pg), *([pool_vt] * pg))


def _mla_sample_kernel(pt_ref, qabs_ref, qrope_ref, cn_ref, rn_ref, wuv_ref, *rest, pg):
    c_refs, r_refs = rest[:pg], rest[pg:2 * pg]
    o_ref, m_ref, l_ref, acc_ref = rest[2 * pg:]
    g = pl.program_id(1)
    t = qabs_ref.shape[0]
    rows = B_HEADS * t
    qs, qr = _stack_mla_q(qabs_ref, qrope_ref)

    @pl.when(g == 0)
    def _():
        _softmax_reset(m_ref, l_ref, acc_ref)
        cn = cn_ref[...].astype(BF16)
        s = _dot_nt(qs, cn) + _dot_nt(qr, rn_ref[...])
        qt = lax.broadcasted_iota(jnp.int32, (rows, t), 0) % t
        kt = lax.broadcasted_iota(jnp.int32, (rows, t), 1)
        _softmax_update(jnp.where(kt <= qt, s, NEG), m_ref, l_ref, acc_ref, cn, rows)

    cc = jnp.concatenate([r[...].astype(BF16) for r in c_refs], axis=0)
    rc = jnp.concatenate([r[...].astype(BF16) for r in r_refs], axis=1)
    s = _dot_nt(qs, cc) + jnp.dot(qr, rc, preferred_element_type=F32)
    _softmax_update(s, m_ref, l_ref, acc_ref, cc, rows)

    @pl.when(g == pl.num_programs(1) - 1)
    def _():
        _mla_finish(acc_ref[...], l_ref[...], wuv_ref, o_ref, t)


def _mla_sample(qabs, qrope, c_new, r_new, pool_c, pool_rt, page_table, wuv, layer, pg):
    db, t, _ = qabs.shape
    n_pages = page_table.shape[1]
    rows = B_HEADS * t
    tok = lambda w: pl.BlockSpec((None, t, w), lambda b, g, pt: (b, 0, 0))
    return pl.pallas_call(
        functools.partial(_mla_sample_kernel, pg=pg),
        grid_spec=pltpu.PrefetchScalarGridSpec(
            num_scalar_prefetch=1, grid=(db, n_pages // pg),
            in_specs=[tok(qabs.shape[2]), tok(qrope.shape[2]), tok(B_KV_RANK), tok(B_ROPE_DIM),
                      pl.BlockSpec(wuv.shape, lambda b, g, pt: (0, 0, 0))]
                     + _page_specs(layer, pg, PAGE_SIZE, B_KV_RANK)
                     + _page_specs(layer, pg, B_ROPE_DIM, PAGE_SIZE),
            out_specs=tok(B_WIDTH),
            scratch_shapes=[pltpu.VMEM((rows, 1), F32), pltpu.VMEM((rows, 1), F32),
                            pltpu.VMEM((rows, B_KV_RANK), F32)]),
        out_shape=jax.ShapeDtypeStruct((db, t, B_WIDTH), F32),
        compiler_params=_cp(("arbitrary", "arbitrary")),
        name="mla_sample",
    )(page_table, qabs, qrope, c_new, r_new, wuv, *([pool_c] * pg), *([pool_rt] * pg))


def _prep_w_in(w):
    z = jnp.zeros((w.shape[0], 64), w.dtype)
    return jnp.concatenate(
        [w[:, 0:1536], w[:, 1600:2112], w[:, 2112:2880], w[:, 3008:3264], w[:, 1536:1600], z,
         w[:, 2880:3008]], axis=1).astype(BF16)


def _prep_w_uq(w):
    w = w.reshape(B_Q_RANK, B_HEADS, B_NOPE_DIM + B_ROPE_DIM)
    nope = w[:, :, :B_NOPE_DIM].reshape(B_Q_RANK, B_HEADS * B_NOPE_DIM)
    rope = jnp.pad(w[:, :, B_NOPE_DIM:], ((0, 0), (0, 0), (0, LANES - B_ROPE_DIM)))
    return jnp.concatenate([nope, rope.reshape(B_Q_RANK, B_HEADS * LANES)], axis=1).astype(BF16)


def _pad_rows_64(w, top):
    z = jnp.zeros((64, w.shape[1]), w.dtype)
    return jnp.concatenate([w, z] if top else [z, w], axis=0)


def kernel(x_prompt, x_sample, cache_moba_k, cache_moba_v, cache_mla_latent, cache_mla_rope, state_rwkv_wkv, state_rwkv_shift, page_table, c_prompt, c_sample, w_ada, b_ada, w_in, mla_q_norm, mla_kv_norm, mla_w_uq, mla_w_uk, mla_w_uv, rwkv_mu, rwkv_w0, rwkv_w2, rwkv_a0, rwkv_a2, rwkv_k_k, rwkv_k_a, rwkv_r_k, rwkv_gn_g, rwkv_gn_b, w_out, ln_g, ln_b):
    depth = w_in.shape[0]
    bp, sp, d = x_prompt.shape
    db, ts, _ = x_sample.shape
    n_pages = page_table.shape[1]
    n_pool = cache_moba_k.shape[1]
    past = n_pages * PAGE_SIZE
    alpha = (2 * depth) ** 0.25
    tm_p = min(256, sp)
    rs = db * ts
    tm_s = rs
    pg = min(32, n_pages)

    mods = _ada_all(jnp.concatenate([c_prompt, c_sample], axis=0), w_ada, b_ada)

    pos_p = jnp.arange(sp, dtype=jnp.int32)
    pos_s = jnp.tile(past + jnp.arange(ts, dtype=jnp.int32), db)
    tabs = [(_rope_tables(p, A_ROPE_DIM // 2, ROPE_THETA, A_HEAD_DIM, A_HEADS),
             _rope_tables(p, B_ROPE_DIM // 2, MLA_ROPE_THETA, LANES, 1)) for p in (pos_p, pos_s)]

    pool_kt = jnp.transpose(cache_moba_k, (0, 1, 3, 4, 2)).reshape(depth, n_pool, A_WIDTH, PAGE_SIZE)
    pool_vt = jnp.transpose(cache_moba_v, (0, 1, 3, 4, 2)).reshape(depth, n_pool, A_WIDTH, PAGE_SIZE)
    pool_rt = jnp.transpose(cache_mla_rope, (0, 1, 3, 2))

    xp = x_prompt.reshape(bp * sp, d)
    xs = x_sample.reshape(rs, d)
    zero_state = jnp.zeros((bp, C_WIDTH, C_WIDTH), F32)
    outs = [[] for _ in range(12)]

    for l in range(depth):
        w = _prep_w_in(w_in[l])
        wuq = _prep_w_uq(mla_w_uq[l])
        wuk = jnp.transpose(mla_w_uk[l], (1, 2, 0)).astype(BF16)
        wuv = jnp.transpose(mla_w_uv[l], (1, 0, 2)).astype(BF16)
        qn = mla_q_norm[l].reshape(1, -1)
        kvn = mla_kv_norm[l].reshape(1, -1)
        mu = rwkv_mu[l]
        row = lambda a: a.reshape(1, -1)
        rwkv_params = (row(mu[:3 * C_WIDTH]), row(mu[3 * C_WIDTH:]), row(rwkv_w0[l]),
                       _pad_rows_64(rwkv_w2[l], True).astype(BF16), row(rwkv_a0[l]),
                       _pad_rows_64(rwkv_a2[l], False).astype(BF16), row(rwkv_k_k[l]),
                       row(rwkv_k_a[l]), row(rwkv_r_k[l]), row(rwkv_gn_g[l]), row(rwkv_gn_b[l]))
        wo = w_out[l].astype(BF16)
        lng, lnb = row(ln_g[l]), row(ln_b[l])

        for stream in (0, 1):
            if stream == 0:
                x2d, nb_, t_, tm = xp, bp, sp, tm_p
                mod = mods[l, :bp].reshape(bp, 1, 3 * d)
                tiles_per_mod = sp // tm
                prev = jnp.zeros((bp, 3 * C_WIDTH + LANES), F32)
                s0 = zero_state
            else:
                x2d, nb_, t_, tm = xs, db, ts, tm_s
                mod = jnp.repeat(mods[l, bp:], ts, axis=0).reshape(1, rs, 3 * d)
                tiles_per_mod = 1
                prev = state_rwkv_shift[l]
                s0 = _state_to_bd(state_rwkv_wkv[l])
            shift, scale, gate = mod[..., :d], mod[..., d:2 * d], mod[..., 2 * d:]
            tab_a, tab_b = tabs[stream]
            (qa, ka, va, ga, qabs, qrope, ckv, kr, gb, rkv, wdad, gc) = _in_proj(
                x2d, scale, shift, tiles_per_mod, w, wuq, wuk, qn, kvn, tab_a, tab_b, tm)
            r3 = lambda a: a.reshape(nb_, t_, a.shape[-1])
            if stream == 0:
                oa = _moba_prompt(r3(qa), r3(ka), r3(va))
                ob = _mla_prompt(r3(qabs), r3(qrope), r3(ckv), r3(kr), wuv)
            else:
                oa = _moba_sample(r3(qa), r3(ka), r3(va), pool_kt, pool_vt, page_table, l, pg)
                ob = _mla_sample(r3(qabs), r3(qrope), r3(ckv), r3(kr), cache_mla_latent,
                                 pool_rt, page_table, wuv, l, pg)
            oc, s_new = _rwkv(r3(rkv), r3(wdad), prev[:, None, :3 * C_WIDTH],
                              prev[:, None, 3 * C_WIDTH:], s0, rwkv_params)
            y = _out_proj(x2d, gate, tiles_per_mod, oa.reshape(-1, A_WIDTH), ga,
                          ob.reshape(-1, B_WIDTH), gb, oc.reshape(-1, C_WIDTH), gc, wo, lng, lnb,
                          alpha, tm)
            last = jnp.concatenate([r3(rkv)[:, -1], r3(wdad)[:, -1]], axis=-1)
            leaves = (ka.reshape(nb_, t_, A_HEADS, A_HEAD_DIM), va.reshape(nb_, t_, A_HEADS, A_HEAD_DIM),
                      r3(ckv), r3(kr), _state_from_bd(s_new), last)
            for i, leaf in enumerate(leaves):
                outs[stream * 6 + i].append(leaf)
            if stream == 0:
                xp = y
            else:
                xs = y

    return (xp.reshape(bp, sp, d), xs.reshape(db, ts, d)) + tuple(jnp.stack(o) for o in outs)
```

```python
import functools
import math

import jax
import jax.numpy as jnp
from jax import lax
from jax.experimental import pallas as pl
from jax.experimental.pallas import tpu as pltpu

F32 = jnp.float32
BF16 = jnp.bfloat16
HI = lax.Precision.HIGHEST

D_MODEL = 1024
PAGE_SIZE = 128
LANES = 128
A_HEADS = 4
A_HEAD_DIM = 64
A_WIDTH = 256
A_ROPE_DIM = 16
ROPE_THETA = 500000.0
MOBA_BLOCK = 256
MOBA_TOPK = 3
Q_BLOCK = 128
B_HEADS = 4
B_NOPE_DIM = 128
B_ROPE_DIM = 64
B_V_DIM = 128
B_WIDTH = 512
B_Q_RANK = 256
B_KV_RANK = 256
MLA_ROPE_THETA = 10000.0
MLA_SCALE = (B_NOPE_DIM + B_ROPE_DIM) ** -0.5
C_HEADS = 4
C_HEAD_DIM = 64
C_WIDTH = 256
C_DECAY_RANK = 64
C_A_RANK = 64
GN_EPS = 64e-5
RWKV_CHUNK = 64
RWKV_SUB = 16
RWKV_SEQS_PER_STEP = 4
LN_EPS = 1e-5
RMS_EPS = 1e-6
NEG = -1e30

OFF_QA, OFF_KA, OFF_VA, OFF_GA = 0, 256, 512, 768
OFF_CQ, OFF_CKV, OFF_GB, OFF_RKV, OFF_GC, OFF_KR, OFF_WDAD = 1024, 1280, 1536, 2048, 2816, 3072, 3200
IN_COLS_PAD = 3328

VMEM_LIMIT = 56 * 1024 * 1024


def _cp(sem):
    return pltpu.CompilerParams(dimension_semantics=sem, vmem_limit_bytes=VMEM_LIMIT)


def _dot(a, b):
    return jnp.dot(a.astype(BF16), b.astype(BF16), preferred_element_type=F32)


def _dot_nt(a, b):
    return lax.dot_general(a.astype(BF16), b.astype(BF16), (((1,), (1,)), ((), ())),
                           preferred_element_type=F32)


def _doth(a, b):
    return jnp.dot(a, b, precision=HI, preferred_element_type=F32)


def _doth_nt(a, b):
    return lax.dot_general(a, b, (((1,), (1,)), ((), ())), precision=HI, preferred_element_type=F32)


_NN = (((1,), (0,)), ((), ()))
_NT = (((1,), (1,)), ((), ()))


def _split(x):
    hi = x.astype(BF16)
    return hi, (x - hi.astype(F32)).astype(BF16)


def _bdot(a, b, dims=_NN):
    return lax.dot_general(a.astype(BF16), b.astype(BF16), dims, preferred_element_type=F32)


def _sigmoid(x):
    return 1.0 / (1.0 + jnp.exp(-x))


def _silu(x):
    return x * _sigmoid(x)


def _head_mask(rows, width, head, head_dim):
    lane = lax.broadcasted_iota(jnp.int32, (rows, width), 1)
    return (lane >= head * head_dim) & (lane < (head + 1) * head_dim)


def _ada_kernel(c_ref, w_ref, b_ref, o_ref):
    o_ref[...] = _dot(_silu(c_ref[...]), w_ref[...]) + b_ref[...]


def _ada_all(c_all, w_ada, b_ada):
    depth = w_ada.shape[0]
    r = c_all.shape[0]
    nj = w_ada.shape[2] // D_MODEL
    return pl.pallas_call(
        _ada_kernel,
        grid=(depth, nj),
        in_specs=[pl.BlockSpec((r, D_MODEL), lambda l, j: (0, 0)),
                  pl.BlockSpec((None, D_MODEL, D_MODEL), lambda l, j: (l, 0, j)),
                  pl.BlockSpec((None, 1, D_MODEL), lambda l, j: (l, 0, j))],
        out_specs=pl.BlockSpec((None, r, D_MODEL), lambda l, j: (l, 0, j)),
        out_shape=jax.ShapeDtypeStruct((depth, r, w_ada.shape[2]), F32),
        compiler_params=_cp(("arbitrary", "arbitrary")),
        name="adaln_mod",
    )(c_all, w_ada, b_ada.reshape(depth, 1, -1))


def _rope_tables(pos, rot_half, theta, head_width, n_rep):
    inv = jnp.power(theta, -jnp.arange(rot_half, dtype=F32) / rot_half)
    ang = pos.astype(F32)[:, None] * inv[None, :]
    cos, sin = jnp.cos(ang), jnp.sin(ang)
    p = pos.shape[0]
    rest = head_width - 2 * rot_half
    c = jnp.concatenate([cos, cos, jnp.ones((p, rest), F32)], axis=1)
    s1 = jnp.concatenate([-sin, jnp.zeros((p, head_width - rot_half), F32)], axis=1)
    s2 = jnp.concatenate([jnp.zeros((p, rot_half), F32), sin, jnp.zeros((p, rest), F32)], axis=1)
    return jnp.stack([jnp.tile(t, (1, n_rep)) for t in (c, s1, s2)])


def _apply_rope(x, cos, s1, s2, half):
    w = x.shape[-1]
    return x * cos + pltpu.roll(x, w - half, 1) * s1 + pltpu.roll(x, half, 1) * s2


def _rms(x, g):
    return x * lax.rsqrt(jnp.mean(x * x, axis=-1, keepdims=True) + RMS_EPS) * g


def _in_kernel(x_ref, sc_ref, sh_ref, w_ref, wuq_ref, wuk_ref, qn_ref, kvn_ref, ta_ref, tb_ref,
               qa_ref, ka_ref, va_ref, ga_ref, qabs_ref, qrope_ref, ckv_ref, kr_ref, gb_ref,
               rkv_ref, wdad_ref, gc_ref):
    hb = (x_ref[...] * (1.0 + sc_ref[...]) + sh_ref[...]).astype(BF16)

    def proj(off, width):
        return jnp.dot(hb, w_ref[:, off:off + width], preferred_element_type=F32)

    ca, sa1, sa2 = ta_ref[0], ta_ref[1], ta_ref[2]
    cb, sb1, sb2 = tb_ref[0], tb_ref[1], tb_ref[2]
    qa_ref[...] = _apply_rope(proj(OFF_QA, A_WIDTH), ca, sa1, sa2, A_ROPE_DIM // 2)
    ka_ref[...] = _apply_rope(proj(OFF_KA, A_WIDTH), ca, sa1, sa2, A_ROPE_DIM // 2)
    va_ref[...] = proj(OFF_VA, A_WIDTH)
    ga_ref[...] = proj(OFF_GA, A_WIDTH)
    cqn = _rms(proj(OFF_CQ, B_Q_RANK), qn_ref[...])
    qf = _dot(cqn, wuq_ref[...])
    for h in range(B_HEADS):
        qabs_ref[:, h * B_KV_RANK:(h + 1) * B_KV_RANK] = _dot(
            qf[:, h * B_NOPE_DIM:(h + 1) * B_NOPE_DIM], wuk_ref[h])
        base = B_HEADS * B_NOPE_DIM + h * LANES
        qrope_ref[:, h * LANES:(h + 1) * LANES] = _apply_rope(
            qf[:, base:base + LANES], cb, sb1, sb2, B_ROPE_DIM // 2)
    ckv_ref[...] = _rms(proj(OFF_CKV, B_KV_RANK), kvn_ref[...])
    kr_ref[...] = _apply_rope(proj(OFF_KR, LANES), cb, sb1, sb2, B_ROPE_DIM // 2)[:, :B_ROPE_DIM]
    gb_ref[...] = proj(OFF_GB, B_WIDTH)
    rkv_ref[...] = proj(OFF_RKV, 3 * C_WIDTH)
    wdad_ref[...] = proj(OFF_WDAD, LANES)
    gc_ref[...] = proj(OFF_GC, C_WIDTH)


def _in_proj(x2d, scale, shift, tiles_per_mod, w, wuq, wuk, qn, kvn, tab_a, tab_b, tm):
    r = x2d.shape[0]
    n_tiles = r // tm
    tab_tiles = tab_a.shape[1] // tm
    rb = scale.shape[1]
    widths = [A_WIDTH, A_WIDTH, A_WIDTH, A_WIDTH, B_HEADS * B_KV_RANK, B_HEADS * LANES, B_KV_RANK,
              B_ROPE_DIM, B_WIDTH, 3 * C_WIDTH, LANES, C_WIDTH]
    row = lambda wd: pl.BlockSpec((tm, wd), lambda i: (i, 0))
    full = lambda a: pl.BlockSpec(a.shape, lambda i: (0,) * a.ndim)
    mod = pl.BlockSpec((None, rb, D_MODEL), lambda i: (i // tiles_per_mod, 0, 0))
    tab = lambda a: pl.BlockSpec((3, tm, a.shape[2]), lambda i: (0, i % tab_tiles, 0))
    return pl.pallas_call(
        _in_kernel,
        grid=(n_tiles,),
        in_specs=[row(D_MODEL), mod, mod, full(w), full(wuq), full(wuk), full(qn), full(kvn),
                  tab(tab_a), tab(tab_b)],
        out_specs=[row(wd) for wd in widths],
        out_shape=[jax.ShapeDtypeStruct((r, wd), F32) for wd in widths],
        compiler_params=_cp(("arbitrary",)),
        name="in_proj",
    )(x2d, scale, shift, w, wuq, wuk, qn, kvn, tab_a, tab_b)


def _top_select(g, valid, axis):
    n = g.shape[axis]
    idx = lax.broadcasted_iota(jnp.int32, g.shape, axis).astype(F32)
    g = jnp.where(valid, g, -jnp.inf)
    sel = jnp.zeros(g.shape, jnp.bool_)
    for _ in range(min(MOBA_TOPK, n)):
        mx = jnp.max(g, axis=axis, keepdims=True)
        first = jnp.min(jnp.where(g == mx, idx, float(n)), axis=axis, keepdims=True)
        hit = idx == first
        sel = sel | hit
        g = jnp.where(hit, -jnp.inf, g)
    return sel & valid


def _stack_heads_masked(q, n_heads, head_dim):
    r, w = q.shape
    return jnp.concatenate(
        [jnp.where(_head_mask(r, w, h, head_dim), q, 0.0) for h in range(n_heads)], axis=0)


def _merge_heads(acc, n_heads, head_dim):
    r = acc.shape[0] // n_heads
    w = acc.shape[1]
    out = jnp.zeros((r, w), F32)
    for h in range(n_heads):
        out = out + jnp.where(_head_mask(r, w, h, head_dim), acc[h * r:(h + 1) * r], 0.0)
    return out


def _softmax_reset(m_ref, l_ref, acc_ref):
    m_ref[...] = jnp.full(m_ref.shape, NEG, F32)
    l_ref[...] = jnp.zeros(l_ref.shape, F32)
    acc_ref[...] = jnp.zeros(acc_ref.shape, F32)


def _softmax_update(s, m_ref, l_ref, acc_ref, v, group, nt=False):
    ps = []
    for g0 in range(0, s.shape[0], group):
        sl = slice(g0, g0 + group)
        sg = s[sl]
        m_old = m_ref[sl]
        m_new = jnp.maximum(m_old, jnp.max(sg, axis=-1, keepdims=True))
        alpha = jnp.exp(m_old - m_new)
        p = jnp.exp(sg - m_new)
        l_ref[sl] = alpha * l_ref[sl] + jnp.sum(p, axis=-1, keepdims=True)
        m_ref[sl] = m_new
        acc_ref[sl] = alpha * acc_ref[sl]
        ps.append(p.astype(BF16))
    p_all = ps[0] if len(ps) == 1 else jnp.concatenate(ps, axis=0)
    acc_ref[...] += lax.dot_general(p_all, v, _NT if nt else _NN, preferred_element_type=F32)


def _moba_prompt_kernel(q_ref, k_ref, v_ref, o_ref, kmean_ref, vt_ref, bias_ref, acc_ref, *, nb):
    c = pl.program_id(1)
    own = (c * Q_BLOCK) // MOBA_BLOCK
    nbp = kmean_ref.shape[0]
    rows = A_HEADS * Q_BLOCK

    @pl.when(c == 0)
    def _():
        means = [jnp.mean(k_ref[n * MOBA_BLOCK:(n + 1) * MOBA_BLOCK, :], axis=0, keepdims=True)
                 for n in range(nb)]
        pad = [jnp.zeros((nbp - nb, A_WIDTH), F32)] if nbp > nb else []
        kmean_ref[...] = jnp.concatenate(means + pad, axis=0)
        for n in range(nb):
            vt_ref[n] = v_ref[n * MOBA_BLOCK:(n + 1) * MOBA_BLOCK, :].T.astype(BF16)

    qm = _stack_heads_masked(q_ref[...], A_HEADS, A_HEAD_DIM)
    gate_t = _doth_nt(kmean_ref[...], qm)
    blk = lax.broadcasted_iota(jnp.int32, gate_t.shape, 0)
    bias_ref[...] = jnp.where(_top_select(gate_t, blk < own, 0), 0.0, NEG)
    qb = (qm * (A_HEAD_DIM ** -0.5)).astype(BF16)
    acc_ref[...] = jnp.zeros(acc_ref.shape, F32)

    def attend(n, st, m, l):
        m_new = jnp.maximum(m, jnp.max(st, axis=0, keepdims=True))
        alpha = jnp.exp(m - m_new)
        p = jnp.exp(st - m_new)
        l = alpha * l + jnp.sum(p, axis=0, keepdims=True)
        pb = p.astype(BF16)
        for h in range(A_HEADS):
            cols = slice(h * Q_BLOCK, (h + 1) * Q_BLOCK)
            acc_ref[h] = acc_ref[h] * alpha[:, cols] + jnp.dot(
                vt_ref[n, h * A_HEAD_DIM:(h + 1) * A_HEAD_DIM, :], pb[:, cols],
                preferred_element_type=F32)
        return m_new, l

    def scores(n):
        start = pl.multiple_of(n * MOBA_BLOCK, MOBA_BLOCK)
        return _dot_nt(k_ref[pl.ds(start, MOBA_BLOCK), :], qb)

    qpos = c * Q_BLOCK + lax.broadcasted_iota(jnp.int32, (1, rows), 1) % Q_BLOCK
    kpos = own * MOBA_BLOCK + lax.broadcasted_iota(jnp.int32, (MOBA_BLOCK, 1), 0)
    m, l = attend(own, jnp.where(kpos <= qpos, scores(own), NEG),
                  jnp.full((1, rows), NEG, F32), jnp.zeros((1, rows), F32))

    def past_block(n, carry):
        return attend(n, scores(n) + bias_ref[pl.ds(n, 1), :], *carry)

    m, l = lax.fori_loop(0, own, past_block, (m, l))
    inv_l = 1.0 / l
    merged_t = jnp.concatenate(
        [acc_ref[h] * inv_l[:, h * Q_BLOCK:(h + 1) * Q_BLOCK] for h in range(A_HEADS)], axis=0)
    o_ref[...] = merged_t.T


def _moba_prompt(q, k, v):
    b, s, w = q.shape
    nb = s // MOBA_BLOCK
    nbp = -(-nb // 8) * 8
    rows = A_HEADS * Q_BLOCK
    blk = pl.BlockSpec((None, Q_BLOCK, w), lambda i, c: (i, c, 0))
    seq = pl.BlockSpec((None, s, w), lambda i, c: (i, 0, 0))
    return pl.pallas_call(
        functools.partial(_moba_prompt_kernel, nb=nb),
        grid=(b, s // Q_BLOCK),
        in_specs=[blk, seq, seq],
        out_specs=blk,
        out_shape=jax.ShapeDtypeStruct((b, s, w), F32),
        scratch_shapes=[pltpu.VMEM((nbp, w), F32), pltpu.VMEM((nb, w, MOBA_BLOCK), BF16),
                        pltpu.VMEM((nbp, rows), F32), pltpu.VMEM((A_HEADS, A_HEAD_DIM, Q_BLOCK), F32)],
        compiler_params=_cp(("arbitrary", "arbitrary")),
        name="moba_prompt",
    )(q, k, v)


def _stack_mla_q(qabs_ref, qrope_ref):
    qs = jnp.concatenate([qabs_ref[:, h * B_KV_RANK:(h + 1) * B_KV_RANK] for h in range(B_HEADS)], axis=0)
    qr = jnp.concatenate([qrope_ref[:, h * LANES:h * LANES + B_ROPE_DIM] for h in range(B_HEADS)], axis=0)
    return (qs * MLA_SCALE).astype(BF16), (qr * MLA_SCALE).astype(BF16)


def _mla_finish(acc, l, wuv_ref, o_ref, r):
    o = acc / l
    for h in range(B_HEADS):
        o_ref[:, h * B_V_DIM:(h + 1) * B_V_DIM] = _dot(o[h * r:(h + 1) * r], wuv_ref[h])


def _mla_prompt_kernel(qabs_ref, qrope_ref, ckv_ref, kr_ref, wuv_ref, o_ref, ckvt_ref, acc_ref, *, tk):
    c = pl.program_id(1)
    rows = B_HEADS * Q_BLOCK

    @pl.when(c == 0)
    def _():
        for j in range(ckvt_ref.shape[0]):
            ckvt_ref[j] = ckv_ref[j * tk:(j + 1) * tk, :].T.astype(BF16)

    qs, qr = _stack_mla_q(qabs_ref, qrope_ref)
    n_full = (c * Q_BLOCK) // tk
    acc_ref[...] = jnp.zeros(acc_ref.shape, F32)

    def attend(j, m, l, masked):
        start = pl.multiple_of(j * tk, tk)
        st = (_dot_nt(ckv_ref[pl.ds(start, tk), :], qs)
              + _dot_nt(kr_ref[pl.ds(start, tk), :], qr))
        if masked:
            qpos = c * Q_BLOCK + lax.broadcasted_iota(jnp.int32, (1, rows), 1) % Q_BLOCK
            kpos = j * tk + lax.broadcasted_iota(jnp.int32, (tk, 1), 0)
            st = jnp.where(kpos <= qpos, st, NEG)
        m_new = jnp.maximum(m, jnp.max(st, axis=0, keepdims=True))
        alpha = jnp.exp(m - m_new)
        p = jnp.exp(st - m_new)
        l = alpha * l + jnp.sum(p, axis=0, keepdims=True)
        acc_ref[...] = acc_ref[...] * alpha + jnp.dot(ckvt_ref[j], p.astype(BF16),
                                                      preferred_element_type=F32)
        return m_new, l

    m, l = lax.fori_loop(0, n_full, lambda j, ml: attend(j, *ml, False),
                         (jnp.full((1, rows), NEG, F32), jnp.zeros((1, rows), F32)))
    m, l = attend(n_full, m, l, True)
    o_t = acc_ref[...] * (1.0 / l)
    for h in range(B_HEADS):
        o_h = o_t[:, h * Q_BLOCK:(h + 1) * Q_BLOCK].T
        o_ref[:, h * B_V_DIM:(h + 1) * B_V_DIM] = _dot(o_h, wuv_ref[h])


def _mla_prompt(qabs, qrope, ckv, kr, wuv):
    b, s, _ = qabs.shape
    tk = min(256, s)
    blk = lambda w: pl.BlockSpec((None, Q_BLOCK, w), lambda i, c: (i, c, 0))
    seq = lambda w: pl.BlockSpec((None, s, w), lambda i, c: (i, 0, 0))
    return pl.pallas_call(
        functools.partial(_mla_prompt_kernel, tk=tk),
        grid=(b, s // Q_BLOCK),
        in_specs=[blk(qabs.shape[2]), blk(qrope.shape[2]), seq(B_KV_RANK), seq(B_ROPE_DIM),
                  pl.BlockSpec(wuv.shape, lambda i, c: (0, 0, 0))],
        out_specs=blk(B_WIDTH),
        out_shape=jax.ShapeDtypeStruct((b, s, B_WIDTH), F32),
        scratch_shapes=[pltpu.VMEM((s // tk, B_KV_RANK, tk), BF16),
                        pltpu.VMEM((B_KV_RANK, B_HEADS * Q_BLOCK), F32)],
        compiler_params=_cp(("arbitrary", "arbitrary")),
        name="mla_prompt",
    )(qabs, qrope, ckv, kr, wuv)


def _each(f, *lists):
    return [f(*xs) for xs in zip(*lists)]


def _neumann(a, eye, levels):
    t = _each(lambda x: eye - x, a)
    p = _each(lambda x: x.astype(BF16), a)
    for _ in range(levels - 1):
        p = _each(lambda x: _bdot(x, x).astype(BF16), p)
        t = _each(lambda y, x: y + _bdot(y, x), t, p)
    return t


def _unit_lower_inverse(a, eye, chunk, row_t, col_t):
    if chunk <= RWKV_SUB:
        return _neumann(a, eye, max(1, math.ceil(math.log2(chunk))))
    near = (row_t // RWKV_SUB) == (col_t // RWKV_SUB)
    a_d = _each(lambda x: jnp.where(near, x, 0.0), a)
    t_d = _each(lambda x: x.astype(BF16), _neumann(a_d, eye, int(math.log2(RWKV_SUB))))
    n = _each(lambda t, x, xd: _bdot(t, x - xd), t_d, a, a_d)
    t_n = _neumann(n, eye, max(1, math.ceil(math.log2(chunk // RWKV_SUB))))
    return _each(_bdot, t_n, t_d)


def _rwkv_kernel(rkv_ref, wdad_ref, prkv_ref, pwdad_ref, s0_ref, mu_rkv_ref, mu_wdad_ref,
                 w0_ref, w2_ref, a0_ref, a2_ref, kk_ref, ka_ref, rk_ref, gng_ref, gnb_ref,
                 o_ref, s_ref, prev_rkv_ref, prev_wdad_ref, *, chunk, n_seq):
    h_, d_, w_ = C_HEADS, C_HEAD_DIM, C_WIDTH
    n = h_ * chunk

    @pl.when(pl.program_id(1) == 0)
    def _():
        s_ref[...] = s0_ref[...]
        prev_rkv_ref[...] = prkv_ref[...]
        prev_wdad_ref[...] = pwdad_ref[...]

    ri = lax.broadcasted_iota(jnp.int32, (n, n), 0)
    ci = lax.broadcasted_iota(jnp.int32, (n, n), 1)
    same = (ri // chunk) == (ci // chunk)
    row_t, col_t = ri % chunk, ci % chunk
    strict = (same & (row_t > col_t))
    incl = (same & (row_t >= col_t))
    eye = (ri == ci).astype(F32)
    tri = (lax.broadcasted_iota(jnp.int32, (chunk, chunk), 0)
           >= lax.broadcasted_iota(jnp.int32, (chunk, chunk), 1)).astype(BF16)
    seg = ((lax.broadcasted_iota(jnp.int32, (w_, w_), 0) // d_)
           == (lax.broadcasted_iota(jnp.int32, (w_, w_), 1) // d_)).astype(BF16)
    first_row = lax.broadcasted_iota(jnp.int32, (chunk, 1), 0) == 0

    def bd(x):
        return _stack_heads_masked(x, h_, d_).astype(BF16)

    def head_sum(x):
        hi, lo = _split(x)
        return (jnp.dot(hi, seg, preferred_element_type=F32)
                + jnp.dot(lo, seg, preferred_element_type=F32))

    def cumsum_rows(x):
        hi = x.astype(BF16)
        r1 = x - hi.astype(F32)
        mid = r1.astype(BF16)
        lo = (r1 - mid.astype(F32)).astype(BF16)
        return (jnp.dot(tri, hi, preferred_element_type=F32)
                + (jnp.dot(tri, mid, preferred_element_type=F32)
                   + jnp.dot(tri, lo, preferred_element_type=F32)))

    def tn(a, b):
        if a.shape[0] < LANES:
            pad = LANES - a.shape[0]
            a = jnp.concatenate([a, jnp.zeros((pad, a.shape[1]), F32)], axis=0)
            b = jnp.concatenate([b, jnp.zeros((pad, b.shape[1]), BF16)], axis=0)
        return _bdot(a.T, b)

    seqs = list(range(n_seq))
    p_rkv = [rkv_ref[i] for i in seqs]
    p_wdad = [wdad_ref[i] for i in seqs]
    m_rkv = [p + mu_rkv_ref[...] * (jnp.where(first_row, prev_rkv_ref[i], pltpu.roll(p, 1, 0)) - p)
             for i, p in zip(seqs, p_rkv)]
    m_wdad = [p + mu_wdad_ref[...] * (jnp.where(first_row, prev_wdad_ref[i], pltpu.roll(p, 1, 0)) - p)
              for i, p in zip(seqs, p_wdad)]
    r = [x[:, :w_] for x in m_rkv]
    k = [x[:, w_:2 * w_] for x in m_rkv]
    v = [x[:, 2 * w_:] for x in m_rkv]

    def log_decay(x):
        z = -(w0_ref[...] + _dot(jnp.tanh(x), w2_ref[...]))
        softplus = jnp.maximum(z, 0.0) + jnp.log(1.0 + jnp.exp(-jnp.abs(z)))
        return -jnp.exp(-softplus - 0.5)

    lw = _each(log_decay, m_wdad)
    a = _each(lambda x: _sigmoid(a0_ref[...] + _dot(x, a2_ref[...])), m_wdad)
    kk = _each(lambda x: x * kk_ref[...], k)
    kk = _each(lambda x: x / jnp.maximum(jnp.sqrt(head_sum(x * x)), 1e-12), kk)
    k_h = _each(lambda x, y: x * (1.0 + (y - 1.0) * ka_ref[...]), k, a)
    b = _each(lambda x, y: x * y, kk, a)
    bonus = _each(lambda x, y, z: head_sum(x * y * rk_ref[...]) * z, r, k_h, v)

    g = _each(cumsum_rows, lw)
    g_end = [x[chunk - 1:chunk, :] for x in g]
    k_til = _each(lambda x, y, z: bd(x * jnp.exp(y - z)), kk, g, lw)
    r_til = _each(lambda x, y: bd(x * jnp.exp(y)), r, g)
    inv_g = _each(lambda x: jnp.exp(-x), g)
    k_hat = _each(lambda x, y: bd(x * y), k_h, inv_g)
    b_hat = _each(lambda x, y: bd(x * y), b, inv_g)
    to_end = _each(lambda x, y: jnp.exp(x - y), g_end, g)
    k_bar = _each(lambda x, y: bd(x * y), k_h, to_end)
    b_bar = _each(lambda x, y: bd(x * y), b, to_end)
    v_f32 = _each(lambda x: _stack_heads_masked(x, h_, d_), v)
    v_bd = _each(lambda x: x.astype(BF16), v_f32)
    s_f32 = [s_ref[i] for i in seqs]
    s_bd = _each(lambda x: x.astype(BF16), s_f32)

    a_b = _each(lambda x, y: jnp.where(strict, _bdot(x, y, _NT), 0.0), k_til, b_hat)
    a_k = _each(lambda x, y: jnp.where(strict, _bdot(x, y, _NT), 0.0), k_til, k_hat)
    a_rk = _each(lambda x, y: jnp.where(incl, _bdot(x, y, _NT), 0.0), r_til, k_hat)
    a_rb = _each(lambda x, y: jnp.where(incl, _bdot(x, y, _NT), 0.0), r_til, b_hat)
    t_inv = _unit_lower_inverse(a_b, eye, chunk, row_t, col_t)
    rhs = _each(lambda x, y, z, w: _bdot(x, y, _NT) + _bdot(z, w), k_til, s_bd, a_k, v_bd)
    u_f32 = _each(_bdot, t_inv, rhs)
    u = _each(lambda x: x.astype(BF16), u_f32)
    o_bd = _each(lambda x, y, z, w, p, q: _bdot(x, y, _NT) + _bdot(z, w) - _bdot(p, q),
                 r_til, s_bd, a_rk, v_bd, a_rb, u)
    s_new = _each(lambda x, y, vf, kb, uf, bb: x * jnp.exp(y) + tn(vf, kb) - tn(uf, bb),
                  s_f32, g_end, v_f32, k_bar, u_f32, b_bar)

    def group_norm(o):
        out = o[0:chunk]
        for h in range(1, h_):
            out = out + o[h * chunk:(h + 1) * chunk]
        cen = out - head_sum(out) * (1.0 / d_)
        var = head_sum(cen * cen) * (1.0 / d_)
        return cen * lax.rsqrt(var + GN_EPS) * gng_ref[...] + gnb_ref[...]

    out = _each(group_norm, o_bd)
    for i in seqs:
        s_ref[i] = s_new[i]
        o_ref[i] = out[i] + bonus[i]
        prev_rkv_ref[i] = p_rkv[i][chunk - 1:chunk, :]
        prev_wdad_ref[i] = p_wdad[i][chunk - 1:chunk, :]


def _rwkv(rkv, wdad, prev_rkv, prev_wdad, s_bd, params):
    b, t, _ = rkv.shape
    chunk = min(RWKV_CHUNK, t)
    n_seq = next(n for n in (RWKV_SEQS_PER_STEP, 2, 1) if b % n == 0)
    seq = lambda w: pl.BlockSpec((n_seq, chunk, w), lambda i, c: (i, c, 0))
    one = lambda w: pl.BlockSpec((n_seq, 1, w), lambda i, c: (i, 0, 0))
    full = lambda a: pl.BlockSpec(a.shape, lambda i, c: (0,) * a.ndim)
    st = pl.BlockSpec((n_seq, C_WIDTH, C_WIDTH), lambda i, c: (i, 0, 0))
    return pl.pallas_call(
        functools.partial(_rwkv_kernel, chunk=chunk, n_seq=n_seq),
        grid=(b // n_seq, t // chunk),
        in_specs=[seq(3 * C_WIDTH), seq(LANES), one(3 * C_WIDTH), one(LANES), st]
                 + [full(p) for p in params],
        out_specs=[seq(C_WIDTH), st],
        out_shape=[jax.ShapeDtypeStruct((b, t, C_WIDTH), F32),
                   jax.ShapeDtypeStruct((b, C_WIDTH, C_WIDTH), F32)],
        scratch_shapes=[pltpu.VMEM((n_seq, 1, 3 * C_WIDTH), F32), pltpu.VMEM((n_seq, 1, LANES), F32)],
        compiler_params=_cp(("arbitrary", "arbitrary")),
        name="rwkv7",
    )(rkv, wdad, prev_rkv, prev_wdad, s_bd, *params)


def _state_to_bd(state):
    b = state.shape[0]
    z = jnp.zeros((b, C_HEAD_DIM, C_HEAD_DIM), state.dtype)
    rows = [jnp.concatenate([state[:, h] if g == h else z for g in range(C_HEADS)], axis=2)
            for h in range(C_HEADS)]
    return jnp.concatenate(rows, axis=1)


def _state_from_bd(s_bd):
    d = C_HEAD_DIM
    return jnp.stack([s_bd[:, h * d:(h + 1) * d, h * d:(h + 1) * d] for h in range(C_HEADS)], axis=1)


def _out_kernel(x_ref, gate_ref, oa_ref, ga_ref, ob_ref, gb_ref, oc_ref, gc_ref, w_ref, lng_ref,
                lnb_ref, y_ref, *, alpha):
    mixed = (jnp.dot((oa_ref[...] * _silu(ga_ref[...])).astype(BF16), w_ref[0:A_WIDTH, :],
                     preferred_element_type=F32)
             + jnp.dot((ob_ref[...] * _silu(gb_ref[...])).astype(BF16),
                       w_ref[A_WIDTH:A_WIDTH + B_WIDTH, :], preferred_element_type=F32)
             + jnp.dot((oc_ref[...] * _silu(gc_ref[...])).astype(BF16),
                       w_ref[A_WIDTH + B_WIDTH:, :], preferred_element_type=F32))
    z = alpha * x_ref[...] + gate_ref[...] * mixed
    mu = jnp.mean(z, axis=-1, keepdims=True)
    zc = z - mu
    var = jnp.mean(zc * zc, axis=-1, keepdims=True)
    y_ref[...] = zc * lax.rsqrt(var + LN_EPS) * lng_ref[...] + lnb_ref[...]


def _out_proj(x2d, gate, tiles_per_mod, oa, ga, ob, gb, oc, gc, w_out, ln_g, ln_b, alpha, tm):
    r = x2d.shape[0]
    rb = gate.shape[1]
    row = lambda wd: pl.BlockSpec((tm, wd), lambda i: (i, 0))
    full = lambda a: pl.BlockSpec(a.shape, lambda i: (0,) * a.ndim)
    mod = pl.BlockSpec((None, rb, D_MODEL), lambda i: (i // tiles_per_mod, 0, 0))
    return pl.pallas_call(
        functools.partial(_out_kernel, alpha=alpha),
        grid=(r // tm,),
        in_specs=[row(D_MODEL), mod, row(A_WIDTH), row(A_WIDTH), row(B_WIDTH), row(B_WIDTH),
                  row(C_WIDTH), row(C_WIDTH), full(w_out), full(ln_g), full(ln_b)],
        out_specs=row(D_MODEL),
        out_shape=jax.ShapeDtypeStruct((r, D_MODEL), F32),
        compiler_params=_cp(("arbitrary",)),
        name="out_proj",
    )(x2d, gate, oa, ga, ob, gb, oc, gc, w_out, ln_g, ln_b)


def _page_specs(layer, pg, rows, cols):
    return [pl.BlockSpec((None, None, rows, cols),
                         lambda b, g, pt, j=j: (layer, pt[b, g * pg + j], 0, 0)) for j in range(pg)]


def _moba_select_kernel(pt_ref, q_ref, *rest, pg, n_blk):
    k_refs, bias_ref, kmean_ref = rest[:pg], rest[pg], rest[pg + 1]
    g = pl.program_id(1)
    ng = pl.num_programs(1)
    per = MOBA_BLOCK // PAGE_SIZE
    bpg = pg // per

    @pl.when(g == 0)
    def _():
        kmean_ref[...] = jnp.zeros(kmean_ref.shape, F32)

    lane = lax.broadcasted_iota(jnp.int32, kmean_ref.shape, 1)
    upd = jnp.zeros(kmean_ref.shape, F32)
    for i in range(bpg):
        blk_sum = k_refs[i * per][...]
        for j in range(1, per):
            blk_sum = blk_sum + k_refs[i * per + j][...]
        col = jnp.sum(blk_sum, axis=1, keepdims=True) * (1.0 / MOBA_BLOCK)
        upd = jnp.where(lane == g * bpg + i, col, upd)
    kmean_ref[...] += upd

    @pl.when(g == ng - 1)
    def _():
        qm = _stack_heads_masked(q_ref[...], A_HEADS, A_HEAD_DIM)
        gate = _doth(qm, kmean_ref[...])
        blk = lax.broadcasted_iota(jnp.int32, gate.shape, 1)
        bias = jnp.where(_top_select(gate, blk < n_blk, 1), 0.0, NEG)
        r_i = lax.broadcasted_iota(jnp.int32, (LANES, LANES), 0)
        c_i = lax.broadcasted_iota(jnp.int32, (LANES, LANES), 1)
        for gg in range(bias_ref.shape[0]):
            pick = ((r_i == gg * bpg + c_i) & (c_i < bpg)).astype(BF16)
            bias_ref[gg] = _dot(bias, pick)


def _moba_attend_kernel(pt_ref, q_ref, kn_ref, vn_ref, bias_ref, expand_ref, *rest, pg):
    k_refs, v_refs = rest[:pg], rest[pg:2 * pg]
    o_ref, m_ref, l_ref, acc_ref = rest[2 * pg:]
    g = pl.program_id(1)
    t = q_ref.shape[0]
    rows = A_HEADS * t
    bpg = expand_ref.shape[0]
    qb = (_stack_heads_masked(q_ref[...], A_HEADS, A_HEAD_DIM) * (A_HEAD_DIM ** -0.5)).astype(BF16)

    @pl.when(g == 0)
    def _():
        _softmax_reset(m_ref, l_ref, acc_ref)
        s = _dot_nt(qb, kn_ref[...])
        qt = lax.broadcasted_iota(jnp.int32, (rows, t), 0) % t
        kt = lax.broadcasted_iota(jnp.int32, (rows, t), 1)
        _softmax_update(jnp.where(kt <= qt, s, NEG), m_ref, l_ref, acc_ref,
                        vn_ref[...].astype(BF16), rows)

    kc = jnp.concatenate([r[...].astype(BF16) for r in k_refs], axis=1)
    vc = jnp.concatenate([r[...].astype(BF16) for r in v_refs], axis=1)
    s = (jnp.dot(qb, kc, preferred_element_type=F32)
         + jnp.dot(bias_ref[:, :bpg].astype(BF16), expand_ref[...], preferred_element_type=F32))
    _softmax_update(s, m_ref, l_ref, acc_ref, vc, rows, nt=True)

    @pl.when(g == pl.num_programs(1) - 1)
    def _():
        o_ref[...] = _merge_heads(acc_ref[...] / l_ref[...], A_HEADS, A_HEAD_DIM)


def _moba_sample(q, k_new, v_new, pool_kt, pool_vt, page_table, layer, pg):
    db, t, w = q.shape
    n_pages = page_table.shape[1]
    n_blk = n_pages * PAGE_SIZE // MOBA_BLOCK
    assert n_blk <= LANES
    ng = n_pages // pg
    bpg = pg * PAGE_SIZE // MOBA_BLOCK
    rows = A_HEADS * t
    tok = pl.BlockSpec((None, t, w), lambda b, g, pt: (b, 0, 0))
    pages = lambda: _page_specs(layer, pg, w, PAGE_SIZE)
    bias = pl.pallas_call(
        functools.partial(_moba_select_kernel, pg=pg, n_blk=n_blk),
        grid_spec=pltpu.PrefetchScalarGridSpec(
            num_scalar_prefetch=1, grid=(db, ng),
            in_specs=[tok] + pages(),
            out_specs=pl.BlockSpec((None, ng, rows, LANES), lambda b, g, pt: (b, 0, 0, 0)),
            scratch_shapes=[pltpu.VMEM((w, LANES), F32)]),
        out_shape=jax.ShapeDtypeStruct((db, ng, rows, LANES), F32),
        compiler_params=_cp(("arbitrary", "arbitrary")),
        name="moba_sample_select",
    )(page_table, q, *([pool_kt] * pg))
    key_blk = jnp.arange(pg * PAGE_SIZE, dtype=jnp.int32) // MOBA_BLOCK
    expand = (jnp.arange(bpg, dtype=jnp.int32)[:, None] == key_blk[None, :]).astype(BF16)
    return pl.pallas_call(
        functools.partial(_moba_attend_kernel, pg=pg),
        grid_spec=pltpu.PrefetchScalarGridSpec(
            num_scalar_prefetch=1, grid=(db, ng),
            in_specs=[tok, tok, tok,
                      pl.BlockSpec((None, None, rows, LANES), lambda b, g, pt: (b, g, 0, 0)),
                      pl.BlockSpec(expand.shape, lambda b, g, pt: (0, 0))] + pages() + pages(),
            out_specs=tok,
            scratch_shapes=[pltpu.VMEM((rows, 1), F32), pltpu.VMEM((rows, 1), F32),
                            pltpu.VMEM((rows, w), F32)]),
        out_shape=jax.ShapeDtypeStruct((db, t, w), F32),
        compiler_params=_cp(("arbitrary", "arbitrary")),
        name="moba_sample_attend",
    )(page_table, q, k_new, v_new, bias, expand, *([pool_kt] * pg), *([pool_vt] * pg))


def _mla_sample_kernel(pt_ref, qabs_ref, qrope_ref, cn_ref, rn_ref, wuv_ref, *rest, pg):
    c_refs, r_refs = rest[:pg], rest[pg:2 * pg]
    o_ref, m_ref, l_ref, acc_ref = rest[2 * pg:]
    g = pl.program_id(1)
    t = qabs_ref.shape[0]
    rows = B_HEADS * t
    qs, qr = _stack_mla_q(qabs_ref, qrope_ref)

    @pl.when(g == 0)
    def _():
        _softmax_reset(m_ref, l_ref, acc_ref)
        cn = cn_ref[...].astype(BF16)
        s = _dot_nt(qs, cn) + _dot_nt(qr, rn_ref[...])
        qt = lax.broadcasted_iota(jnp.int32, (rows, t), 0) % t
        kt = lax.broadcasted_iota(jnp.int32, (rows, t), 1)
        _softmax_update(jnp.where(kt <= qt, s, NEG), m_ref, l_ref, acc_ref, cn, rows)

    cc = jnp.concatenate([r[...].astype(BF16) for r in c_refs], axis=0)
    rc = jnp.concatenate([r[...].astype(BF16) for r in r_refs], axis=1)
    s = _dot_nt(qs, cc) + jnp.dot(qr, rc, preferred_element_type=F32)
    _softmax_update(s, m_ref, l_ref, acc_ref, cc, rows)

    @pl.when(g == pl.num_programs(1) - 1)
    def _():
        _mla_finish(acc_ref[...], l_ref[...], wuv_ref, o_ref, t)


def _mla_sample(qabs, qrope, c_new, r_new, pool_c, pool_rt, page_table, wuv, layer, pg):
    db, t, _ = qabs.shape
    n_pages = page_table.shape[1]
    rows = B_HEADS * t
    tok = lambda w: pl.BlockSpec((None, t, w), lambda b, g, pt: (b, 0, 0))
    return pl.pallas_call(
        functools.partial(_mla_sample_kernel, pg=pg),
        grid_spec=pltpu.PrefetchScalarGridSpec(
            num_scalar_prefetch=1, grid=(db, n_pages // pg),
            in_specs=[tok(qabs.shape[2]), tok(qrope.shape[2]), tok(B_KV_RANK), tok(B_ROPE_DIM),
                      pl.BlockSpec(wuv.shape, lambda b, g, pt: (0, 0, 0))]
                     + _page_specs(layer, pg, PAGE_SIZE, B_KV_RANK)
                     + _page_specs(layer, pg, B_ROPE_DIM, PAGE_SIZE),
            out_specs=tok(B_WIDTH),
            scratch_shapes=[pltpu.VMEM((rows, 1), F32), pltpu.VMEM((rows, 1), F32),
                            pltpu.VMEM((rows, B_KV_RANK), F32)]),
        out_shape=jax.ShapeDtypeStruct((db, t, B_WIDTH), F32),
        compiler_params=_cp(("arbitrary", "arbitrary")),
        name="mla_sample",
    )(page_table, qabs, qrope, c_new, r_new, wuv, *([pool_c] * pg), *([pool_rt] * pg))


def _prep_w_in(w):
    z = jnp.zeros((w.shape[0], 64), w.dtype)
    return jnp.concatenate(
        [w[:, 0:1536], w[:, 1600:2112], w[:, 2112:2880], w[:, 3008:3264], w[:, 1536:1600], z,
         w[:, 2880:3008]], axis=1).astype(BF16)


def _prep_w_uq(w):
    w = w.reshape(B_Q_RANK, B_HEADS, B_NOPE_DIM + B_ROPE_DIM)
    nope = w[:, :, :B_NOPE_DIM].reshape(B_Q_RANK, B_HEADS * B_NOPE_DIM)
    rope = jnp.pad(w[:, :, B_NOPE_DIM:], ((0, 0), (0, 0), (0, LANES - B_ROPE_DIM)))
    return jnp.concatenate([nope, rope.reshape(B_Q_RANK, B_HEADS * LANES)], axis=1).astype(BF16)


def _pad_rows_64(w, top):
    z = jnp.zeros((64, w.shape[1]), w.dtype)
    return jnp.concatenate([w, z] if top else [z, w], axis=0)


def kernel(x_prompt, x_sample, cache_moba_k, cache_moba_v, cache_mla_latent, cache_mla_rope, state_rwkv_wkv, state_rwkv_shift, page_table, c_prompt, c_sample, w_ada, b_ada, w_in, mla_q_norm, mla_kv_norm, mla_w_uq, mla_w_uk, mla_w_uv, rwkv_mu, rwkv_w0, rwkv_w2, rwkv_a0, rwkv_a2, rwkv_k_k, rwkv_k_a, rwkv_r_k, rwkv_gn_g, rwkv_gn_b, w_out, ln_g, ln_b):
    depth = w_in.shape[0]
    bp, sp, d = x_prompt.shape
    db, ts, _ = x_sample.shape
    n_pages = page_table.shape[1]
    n_pool = cache_moba_k.shape[1]
    past = n_pages * PAGE_SIZE
    alpha = (2 * depth) ** 0.25
    tm_p = min(256, sp)
    rs = db * ts
    tm_s = rs
    pg = min(32, n_pages)

    mods = _ada_all(jnp.concatenate([c_prompt, c_sample], axis=0), w_ada, b_ada)

    pos_p = jnp.arange(sp, dtype=jnp.int32)
    pos_s = jnp.tile(past + jnp.arange(ts, dtype=jnp.int32), db)
    tabs = [(_rope_tables(p, A_ROPE_DIM // 2, ROPE_THETA, A_HEAD_DIM, A_HEADS),
             _rope_tables(p, B_ROPE_DIM // 2, MLA_ROPE_THETA, LANES, 1)) for p in (pos_p, pos_s)]

    pool_kt = jnp.transpose(cache_moba_k, (0, 1, 3, 4, 2)).reshape(depth, n_pool, A_WIDTH, PAGE_SIZE)
    pool_vt = jnp.transpose(cache_moba_v, (0, 1, 3, 4, 2)).reshape(depth, n_pool, A_WIDTH, PAGE_SIZE)
    pool_rt = jnp.transpose(cache_mla_rope, (0, 1, 3, 2))

    xp = x_prompt.reshape(bp * sp, d)
    xs = x_sample.reshape(rs, d)
    zero_state = jnp.zeros((bp, C_WIDTH, C_WIDTH), F32)
    outs = [[] for _ in range(12)]

    for l in range(depth):
        w = _prep_w_in(w_in[l])
        wuq = _prep_w_uq(mla_w_uq[l])
        wuk = jnp.transpose(mla_w_uk[l], (1, 2, 0)).astype(BF16)
        wuv = jnp.transpose(mla_w_uv[l], (1, 0, 2)).astype(BF16)
        qn = mla_q_norm[l].reshape(1, -1)
        kvn = mla_kv_norm[l].reshape(1, -1)
        mu = rwkv_mu[l]
        row = lambda a: a.reshape(1, -1)
        rwkv_params = (row(mu[:3 * C_WIDTH]), row(mu[3 * C_WIDTH:]), row(rwkv_w0[l]),
                       _pad_rows_64(rwkv_w2[l], True).astype(BF16), row(rwkv_a0[l]),
                       _pad_rows_64(rwkv_a2[l], False).astype(BF16), row(rwkv_k_k[l]),
                       row(rwkv_k_a[l]), row(rwkv_r_k[l]), row(rwkv_gn_g[l]), row(rwkv_gn_b[l]))
        wo = w_out[l].astype(BF16)
        lng, lnb = row(ln_g[l]), row(ln_b[l])

        for stream in (0, 1):
            if stream == 0:
                x2d, nb_, t_, tm = xp, bp, sp, tm_p
                mod = mods[l, :bp].reshape(bp, 1, 3 * d)
                tiles_per_mod = sp // tm
                prev = jnp.zeros((bp, 3 * C_WIDTH + LANES), F32)
                s0 = zero_state
            else:
                x2d, nb_, t_, tm = xs, db, ts, tm_s
                mod = jnp.repeat(mods[l, bp:], ts, axis=0).reshape(1, rs, 3 * d)
                tiles_per_mod = 1
                prev = state_rwkv_shift[l]
                s0 = _state_to_bd(state_rwkv_wkv[l])
            shift, scale, gate = mod[..., :d], mod[..., d:2 * d], mod[..., 2 * d:]
            tab_a, tab_b = tabs[stream]
            (qa, ka, va, ga, qabs, qrope, ckv, kr, gb, rkv, wdad, gc) = _in_proj(
                x2d, scale, shift, tiles_per_mod, w, wuq, wuk, qn, kvn, tab_a, tab_b, tm)
            r3 = lambda a: a.reshape(nb_, t_, a.shape[-1])
            if stream == 0:
                oa = _moba_prompt(r3(qa), r3(ka), r3(va))
                ob = _mla_prompt(r3(qabs), r3(qrope), r3(ckv), r3(kr), wuv)
            else:
                oa = _moba_sample(r3(qa), r3(ka), r3(va), pool_kt, pool_vt, page_table, l, pg)
                ob = _mla_sample(r3(qabs), r3(qrope), r3(ckv), r3(kr), cache_mla_latent,
                                 pool_rt, page_table, wuv, l, pg)
            oc, s_new = _rwkv(r3(rkv), r3(wdad), prev[:, None, :3 * C_WIDTH],
                              prev[:, None, 3 * C_WIDTH:], s0, rwkv_params)
            y = _out_proj(x2d, gate, tiles_per_mod, oa.reshape(-1, A_WIDTH), ga,
                          ob.reshape(-1, B_WIDTH), gb, oc.reshape(-1, C_WIDTH), gc, wo, lng, lnb,
                          alpha, tm)
            last = jnp.concatenate([r3(rkv)[:, -1], r3(wdad)[:, -1]], axis=-1)
            leaves = (ka.reshape(nb_, t_, A_HEADS, A_HEAD_DIM), va.reshape(nb_, t_, A_HEADS, A_HEAD_DIM),
                      r3(ckv), r3(kr), _state_from_bd(s_new), last)
            for i, leaf in enumerate(leaves):
                outs[stream * 6 + i].append(leaf)
            if stream == 0:
                xp = y
            else:
                xs = y

    return (xp.reshape(bp, sp, d), xs.reshape(db, ts, d)) + tuple(jnp.stack(o) for o in outs)
```

```python
import functools
import math

import jax
import jax.numpy as jnp
from jax import lax
from jax.experimental import pallas as pl
from jax.experimental.pallas import tpu as pltpu

F32 = jnp.float32
BF16 = jnp.bfloat16
HI = lax.Precision.HIGHEST

D_MODEL = 1024
PAGE_SIZE = 128
LANES = 128
A_HEADS = 4
A_HEAD_DIM = 64
A_WIDTH = 256
A_ROPE_DIM = 16
ROPE_THETA = 500000.0
MOBA_BLOCK = 256
MOBA_TOPK = 3
Q_BLOCK = 256
B_HEADS = 4
B_NOPE_DIM = 128
B_ROPE_DIM = 64
B_V_DIM = 128
B_WIDTH = 512
B_Q_RANK = 256
B_KV_RANK = 256
MLA_ROPE_THETA = 10000.0
MLA_SCALE = (B_NOPE_DIM + B_ROPE_DIM) ** -0.5
C_HEADS = 4
C_HEAD_DIM = 64
C_WIDTH = 256
C_DECAY_RANK = 64
C_A_RANK = 64
GN_EPS = 64e-5
RWKV_CHUNK = 64
RWKV_SUB = 16
RWKV_SEQS_PER_STEP = 4
LN_EPS = 1e-5
RMS_EPS = 1e-6
NEG = -1e30

OFF_QA, OFF_KA, OFF_VA, OFF_GA = 0, 256, 512, 768
OFF_CQ, OFF_CKV, OFF_GB, OFF_RKV, OFF_GC, OFF_KR, OFF_WDAD = 1024, 1280, 1536, 2048, 2816, 3072, 3200
IN_COLS_PAD = 3328

VMEM_LIMIT = 56 * 1024 * 1024


def _cp(sem):
    return pltpu.CompilerParams(dimension_semantics=sem, vmem_limit_bytes=VMEM_LIMIT)


def _dot(a, b):
    return jnp.dot(a.astype(BF16), b.astype(BF16), preferred_element_type=F32)


def _dot_nt(a, b):
    return lax.dot_general(a.astype(BF16), b.astype(BF16), (((1,), (1,)), ((), ())),
                           preferred_element_type=F32)


def _doth(a, b):
    return jnp.dot(a, b, precision=HI, preferred_element_type=F32)


def _doth_nt(a, b):
    return lax.dot_general(a, b, (((1,), (1,)), ((), ())), precision=HI, preferred_element_type=F32)


_NN = (((1,), (0,)), ((), ()))
_NT = (((1,), (1,)), ((), ()))


def _split(x):
    hi = x.astype(BF16)
    return hi, (x - hi.astype(F32)).astype(BF16)


def _bdot(a, b, dims=_NN):
    return lax.dot_general(a.astype(BF16), b.astype(BF16), dims, preferred_element_type=F32)


def _sigmoid(x):
    return 1.0 / (1.0 + jnp.exp(-x))


def _silu(x):
    return x * _sigmoid(x)


def _head_mask(rows, width, head, head_dim):
    lane = lax.broadcasted_iota(jnp.int32, (rows, width), 1)
    return (lane >= head * head_dim) & (lane < (head + 1) * head_dim)


def _ada_kernel(c_ref, w_ref, b_ref, o_ref):
    o_ref[...] = _dot(_silu(c_ref[...]), w_ref[...]) + b_ref[...]


def _ada_all(c_all, w_ada, b_ada):
    depth = w_ada.shape[0]
    r = c_all.shape[0]
    nj = w_ada.shape[2] // D_MODEL
    return pl.pallas_call(
        _ada_kernel,
        grid=(depth, nj),
        in_specs=[pl.BlockSpec((r, D_MODEL), lambda l, j: (0, 0)),
                  pl.BlockSpec((None, D_MODEL, D_MODEL), lambda l, j: (l, 0, j)),
                  pl.BlockSpec((None, 1, D_MODEL), lambda l, j: (l, 0, j))],
        out_specs=pl.BlockSpec((None, r, D_MODEL), lambda l, j: (l, 0, j)),
        out_shape=jax.ShapeDtypeStruct((depth, r, w_ada.shape[2]), F32),
        compiler_params=_cp(("arbitrary", "arbitrary")),
        name="adaln_mod",
    )(c_all, w_ada, b_ada.reshape(depth, 1, -1))


def _rope_tables(pos, rot_half, theta, head_width, n_rep):
    inv = jnp.power(theta, -jnp.arange(rot_half, dtype=F32) / rot_half)
    ang = pos.astype(F32)[:, None] * inv[None, :]
    cos, sin = jnp.cos(ang), jnp.sin(ang)
    p = pos.shape[0]
    rest = head_width - 2 * rot_half
    c = jnp.concatenate([cos, cos, jnp.ones((p, rest), F32)], axis=1)
    s1 = jnp.concatenate([-sin, jnp.zeros((p, head_width - rot_half), F32)], axis=1)
    s2 = jnp.concatenate([jnp.zeros((p, rot_half), F32), sin, jnp.zeros((p, rest), F32)], axis=1)
    return jnp.stack([jnp.tile(t, (1, n_rep)) for t in (c, s1, s2)])


def _apply_rope(x, cos, s1, s2, half):
    w = x.shape[-1]
    return x * cos + pltpu.roll(x, w - half, 1) * s1 + pltpu.roll(x, half, 1) * s2


def _rms(x, g):
    return x * lax.rsqrt(jnp.mean(x * x, axis=-1, keepdims=True) + RMS_EPS) * g


def _in_kernel(x_ref, sc_ref, sh_ref, w_ref, wuq_ref, wuk_ref, qn_ref, kvn_ref, ta_ref, tb_ref,
               qa_ref, ka_ref, va_ref, ga_ref, qabs_ref, qrope_ref, ckv_ref, kr_ref, gb_ref,
               rkv_ref, wdad_ref, gc_ref):
    hb = (x_ref[...] * (1.0 + sc_ref[...]) + sh_ref[...]).astype(BF16)

    def proj(off, width):
        return jnp.dot(hb, w_ref[:, off:off + width], preferred_element_type=F32)

    ca, sa1, sa2 = ta_ref[0], ta_ref[1], ta_ref[2]
    cb, sb1, sb2 = tb_ref[0], tb_ref[1], tb_ref[2]
    qa_ref[...] = _apply_rope(proj(OFF_QA, A_WIDTH), ca, sa1, sa2, A_ROPE_DIM // 2)
    ka_ref[...] = _apply_rope(proj(OFF_KA, A_WIDTH), ca, sa1, sa2, A_ROPE_DIM // 2)
    va_ref[...] = proj(OFF_VA, A_WIDTH)
    ga_ref[...] = proj(OFF_GA, A_WIDTH)
    cqn = _rms(proj(OFF_CQ, B_Q_RANK), qn_ref[...])
    qf = _dot(cqn, wuq_ref[...])
    for h in range(B_HEADS):
        qabs_ref[:, h * B_KV_RANK:(h + 1) * B_KV_RANK] = _dot(
            qf[:, h * B_NOPE_DIM:(h + 1) * B_NOPE_DIM], wuk_ref[h])
        base = B_HEADS * B_NOPE_DIM + h * LANES
        qrope_ref[:, h * LANES:(h + 1) * LANES] = _apply_rope(
            qf[:, base:base + LANES], cb, sb1, sb2, B_ROPE_DIM // 2)
    ckv_ref[...] = _rms(proj(OFF_CKV, B_KV_RANK), kvn_ref[...])
    kr_ref[...] = _apply_rope(proj(OFF_KR, LANES), cb, sb1, sb2, B_ROPE_DIM // 2)[:, :B_ROPE_DIM]
    gb_ref[...] = proj(OFF_GB, B_WIDTH)
    rkv_ref[...] = proj(OFF_RKV, 3 * C_WIDTH)
    wdad_ref[...] = proj(OFF_WDAD, LANES)
    gc_ref[...] = proj(OFF_GC, C_WIDTH)


def _in_proj(x2d, scale, shift, tiles_per_mod, w, wuq, wuk, qn, kvn, tab_a, tab_b, tm):
    r = x2d.shape[0]
    n_tiles = r // tm
    tab_tiles = tab_a.shape[1] // tm
    rb = scale.shape[1]
    widths = [A_WIDTH, A_WIDTH, A_WIDTH, A_WIDTH, B_HEADS * B_KV_RANK, B_HEADS * LANES, B_KV_RANK,
              B_ROPE_DIM, B_WIDTH, 3 * C_WIDTH, LANES, C_WIDTH]
    row = lambda wd: pl.BlockSpec((tm, wd), lambda i: (i, 0))
    full = lambda a: pl.BlockSpec(a.shape, lambda i: (0,) * a.ndim)
    mod = pl.BlockSpec((None, rb, D_MODEL), lambda i: (i // tiles_per_mod, 0, 0))
    tab = lambda a: pl.BlockSpec((3, tm, a.shape[2]), lambda i: (0, i % tab_tiles, 0))
    return pl.pallas_call(
        _in_kernel,
        grid=(n_tiles,),
        in_specs=[row(D_MODEL), mod, mod, full(w), full(wuq), full(wuk), full(qn), full(kvn),
                  tab(tab_a), tab(tab_b)],
        out_specs=[row(wd) for wd in widths],
        out_shape=[jax.ShapeDtypeStruct((r, wd), F32) for wd in widths],
        compiler_params=_cp(("arbitrary",)),
        name="in_proj",
    )(x2d, scale, shift, w, wuq, wuk, qn, kvn, tab_a, tab_b)


def _top_select(g, valid, axis):
    n = g.shape[axis]
    idx = lax.broadcasted_iota(jnp.int32, g.shape, axis).astype(F32)
    g = jnp.where(valid, g, -jnp.inf)
    sel = jnp.zeros(g.shape, jnp.bool_)
    for _ in range(min(MOBA_TOPK, n)):
        mx = jnp.max(g, axis=axis, keepdims=True)
        first = jnp.min(jnp.where(g == mx, idx, float(n)), axis=axis, keepdims=True)
        hit = idx == first
        sel = sel | hit
        g = jnp.where(hit, -jnp.inf, g)
    return sel & valid


def _stack_heads_masked(q, n_heads, head_dim):
    r, w = q.shape
    return jnp.concatenate(
        [jnp.where(_head_mask(r, w, h, head_dim), q, 0.0) for h in range(n_heads)], axis=0)


def _merge_heads(acc, n_heads, head_dim):
    r = acc.shape[0] // n_heads
    w = acc.shape[1]
    out = jnp.zeros((r, w), F32)
    for h in range(n_heads):
        out = out + jnp.where(_head_mask(r, w, h, head_dim), acc[h * r:(h + 1) * r], 0.0)
    return out


def _softmax_reset(m_ref, l_ref, acc_ref):
    m_ref[...] = jnp.full(m_ref.shape, NEG, F32)
    l_ref[...] = jnp.zeros(l_ref.shape, F32)
    acc_ref[...] = jnp.zeros(acc_ref.shape, F32)


def _softmax_update(s, m_ref, l_ref, acc_ref, v, group, nt=False):
    ps = []
    for g0 in range(0, s.shape[0], group):
        sl = slice(g0, g0 + group)
        sg = s[sl]
        m_old = m_ref[sl]
        m_new = jnp.maximum(m_old, jnp.max(sg, axis=-1, keepdims=True))
        alpha = jnp.exp(m_old - m_new)
        p = jnp.exp(sg - m_new)
        l_ref[sl] = alpha * l_ref[sl] + jnp.sum(p, axis=-1, keepdims=True)
        m_ref[sl] = m_new
        acc_ref[sl] = alpha * acc_ref[sl]
        ps.append(p.astype(BF16))
    p_all = ps[0] if len(ps) == 1 else jnp.concatenate(ps, axis=0)
    acc_ref[...] += lax.dot_general(p_all, v, _NT if nt else _NN, preferred_element_type=F32)


def _moba_prompt_kernel(q_ref, k_ref, v_ref, o_ref, kmean_ref, vt_ref, bias_ref, acc_ref, *, nb):
    c = pl.program_id(1)
    own = (c * Q_BLOCK) // MOBA_BLOCK
    nbp = kmean_ref.shape[0]
    rows = A_HEADS * Q_BLOCK

    @pl.when(c == 0)
    def _():
        means = [jnp.mean(k_ref[n * MOBA_BLOCK:(n + 1) * MOBA_BLOCK, :], axis=0, keepdims=True)
                 for n in range(nb)]
        pad = [jnp.zeros((nbp - nb, A_WIDTH), F32)] if nbp > nb else []
        kmean_ref[...] = jnp.concatenate(means + pad, axis=0)
        for n in range(nb):
            vt_ref[n] = v_ref[n * MOBA_BLOCK:(n + 1) * MOBA_BLOCK, :].T.astype(BF16)

    qm = _stack_heads_masked(q_ref[...], A_HEADS, A_HEAD_DIM)
    gate_t = _doth_nt(kmean_ref[...], qm)
    blk = lax.broadcasted_iota(jnp.int32, gate_t.shape, 0)
    bias_ref[...] = jnp.where(_top_select(gate_t, blk < own, 0), 0.0, NEG)
    qb = (qm * (A_HEAD_DIM ** -0.5)).astype(BF16)
    acc_ref[...] = jnp.zeros(acc_ref.shape, F32)

    def attend(n, st, m, l):
        m_new = jnp.maximum(m, jnp.max(st, axis=0, keepdims=True))
        alpha = jnp.exp(m - m_new)
        p = jnp.exp(st - m_new)
        l = alpha * l + jnp.sum(p, axis=0, keepdims=True)
        pb = p.astype(BF16)
        for h in range(A_HEADS):
            cols = slice(h * Q_BLOCK, (h + 1) * Q_BLOCK)
            acc_ref[h] = acc_ref[h] * alpha[:, cols] + jnp.dot(
                vt_ref[n, h * A_HEAD_DIM:(h + 1) * A_HEAD_DIM, :], pb[:, cols],
                preferred_element_type=F32)
        return m_new, l

    def scores(n):
        start = pl.multiple_of(n * MOBA_BLOCK, MOBA_BLOCK)
        return _dot_nt(k_ref[pl.ds(start, MOBA_BLOCK), :], qb)

    qpos = c * Q_BLOCK + lax.broadcasted_iota(jnp.int32, (1, rows), 1) % Q_BLOCK
    kpos = own * MOBA_BLOCK + lax.broadcasted_iota(jnp.int32, (MOBA_BLOCK, 1), 0)
    m, l = attend(own, jnp.where(kpos <= qpos, scores(own), NEG),
                  jnp.full((1, rows), NEG, F32), jnp.zeros((1, rows), F32))

    def past_block(n, carry):
        return attend(n, scores(n) + bias_ref[pl.ds(n, 1), :], *carry)

    m, l = lax.fori_loop(0, own, past_block, (m, l))
    inv_l = 1.0 / l
    merged_t = jnp.concatenate(
        [acc_ref[h] * inv_l[:, h * Q_BLOCK:(h + 1) * Q_BLOCK] for h in range(A_HEADS)], axis=0)
    o_ref[...] = merged_t.T


def _moba_prompt(q, k, v):
    b, s, w = q.shape
    nb = s // MOBA_BLOCK
    nbp = -(-nb // 8) * 8
    rows = A_HEADS * Q_BLOCK
    blk = pl.BlockSpec((None, Q_BLOCK, w), lambda i, c: (i, c, 0))
    seq = pl.BlockSpec((None, s, w), lambda i, c: (i, 0, 0))
    return pl.pallas_call(
        functools.partial(_moba_prompt_kernel, nb=nb),
        grid=(b, s // Q_BLOCK),
        in_specs=[blk, seq, seq],
        out_specs=blk,
        out_shape=jax.ShapeDtypeStruct((b, s, w), F32),
        scratch_shapes=[pltpu.VMEM((nbp, w), F32), pltpu.VMEM((nb, w, MOBA_BLOCK), BF16),
                        pltpu.VMEM((nbp, rows), F32), pltpu.VMEM((A_HEADS, A_HEAD_DIM, Q_BLOCK), F32)],
        compiler_params=_cp(("arbitrary", "arbitrary")),
        name="moba_prompt",
    )(q, k, v)


def _stack_mla_q(qabs_ref, qrope_ref):
    qs = jnp.concatenate([qabs_ref[:, h * B_KV_RANK:(h + 1) * B_KV_RANK] for h in range(B_HEADS)], axis=0)
    qr = jnp.concatenate([qrope_ref[:, h * LANES:h * LANES + B_ROPE_DIM] for h in range(B_HEADS)], axis=0)
    return (qs * MLA_SCALE).astype(BF16), (qr * MLA_SCALE).astype(BF16)


def _mla_finish(acc, l, wuv_ref, o_ref, r):
    o = acc / l
    for h in range(B_HEADS):
        o_ref[:, h * B_V_DIM:(h + 1) * B_V_DIM] = _dot(o[h * r:(h + 1) * r], wuv_ref[h])


def _mla_prompt_kernel(qabs_ref, qrope_ref, ckv_ref, kr_ref, wuv_ref, o_ref, ckvt_ref, acc_ref, *, tk):
    c = pl.program_id(1)
    rows = B_HEADS * Q_BLOCK

    @pl.when(c == 0)
    def _():
        for j in range(ckvt_ref.shape[0]):
            ckvt_ref[j] = ckv_ref[j * tk:(j + 1) * tk, :].T.astype(BF16)

    qs, qr = _stack_mla_q(qabs_ref, qrope_ref)
    n_full = (c * Q_BLOCK) // tk
    acc_ref[...] = jnp.zeros(acc_ref.shape, F32)

    def attend(j, m, l, masked):
        start = pl.multiple_of(j * tk, tk)
        st = (_dot_nt(ckv_ref[pl.ds(start, tk), :], qs)
              + _dot_nt(kr_ref[pl.ds(start, tk), :], qr))
        if masked:
            qpos = c * Q_BLOCK + lax.broadcasted_iota(jnp.int32, (1, rows), 1) % Q_BLOCK
            kpos = j * tk + lax.broadcasted_iota(jnp.int32, (tk, 1), 0)
            st = jnp.where(kpos <= qpos, st, NEG)
        m_new = jnp.maximum(m, jnp.max(st, axis=0, keepdims=True))
        alpha = jnp.exp(m - m_new)
        p = jnp.exp(st - m_new)
        l = alpha * l + jnp.sum(p, axis=0, keepdims=True)
        acc_ref[...] = acc_ref[...] * alpha + jnp.dot(ckvt_ref[j], p.astype(BF16),
                                                      preferred_element_type=F32)
        return m_new, l

    m, l = lax.fori_loop(0, n_full, lambda j, ml: attend(j, *ml, False),
                         (jnp.full((1, rows), NEG, F32), jnp.zeros((1, rows), F32)))
    m, l = attend(n_full, m, l, True)
    o_t = acc_ref[...] * (1.0 / l)
    for h in range(B_HEADS):
        o_h = o_t[:, h * Q_BLOCK:(h + 1) * Q_BLOCK].T
        o_ref[:, h * B_V_DIM:(h + 1) * B_V_DIM] = _dot(o_h, wuv_ref[h])


def _mla_prompt(qabs, qrope, ckv, kr, wuv):
    b, s, _ = qabs.shape
    tk = min(256, s)
    blk = lambda w: pl.BlockSpec((None, Q_BLOCK, w), lambda i, c: (i, c, 0))
    seq = lambda w: pl.BlockSpec((None, s, w), lambda i, c: (i, 0, 0))
    return pl.pallas_call(
        functools.partial(_mla_prompt_kernel, tk=tk),
        grid=(b, s // Q_BLOCK),
        in_specs=[blk(qabs.shape[2]), blk(qrope.shape[2]), seq(B_KV_RANK), seq(B_ROPE_DIM),
                  pl.BlockSpec(wuv.shape, lambda i, c: (0, 0, 0))],
        out_specs=blk(B_WIDTH),
        out_shape=jax.ShapeDtypeStruct((b, s, B_WIDTH), F32),
        scratch_shapes=[pltpu.VMEM((s // tk, B_KV_RANK, tk), BF16),
                        pltpu.VMEM((B_KV_RANK, B_HEADS * Q_BLOCK), F32)],
        compiler_params=_cp(("arbitrary", "arbitrary")),
        name="mla_prompt",
    )(qabs, qrope, ckv, kr, wuv)


def _each(f, *lists):
    return [f(*xs) for xs in zip(*lists)]


def _neumann(a, eye, levels):
    t = _each(lambda x: eye - x, a)
    p = _each(lambda x: x.astype(BF16), a)
    for _ in range(levels - 1):
        p = _each(lambda x: _bdot(x, x).astype(BF16), p)
        t = _each(lambda y, x: y + _bdot(y, x), t, p)
    return t


def _unit_lower_inverse(a, eye, chunk, row_t, col_t):
    if chunk <= RWKV_SUB:
        return _neumann(a, eye, max(1, math.ceil(math.log2(chunk))))
    near = (row_t // RWKV_SUB) == (col_t // RWKV_SUB)
    a_d = _each(lambda x: jnp.where(near, x, 0.0), a)
    t_d = _each(lambda x: x.astype(BF16), _neumann(a_d, eye, int(math.log2(RWKV_SUB))))
    n = _each(lambda t, x, xd: _bdot(t, x - xd), t_d, a, a_d)
    t_n = _neumann(n, eye, max(1, math.ceil(math.log2(chunk // RWKV_SUB))))
    return _each(_bdot, t_n, t_d)


def _rwkv_kernel(rkv_ref, wdad_ref, prkv_ref, pwdad_ref, s0_ref, mu_rkv_ref, mu_wdad_ref,
                 w0_ref, w2_ref, a0_ref, a2_ref, kk_ref, ka_ref, rk_ref, gng_ref, gnb_ref,
                 o_ref, s_ref, prev_rkv_ref, prev_wdad_ref, *, chunk, n_seq):
    h_, d_, w_ = C_HEADS, C_HEAD_DIM, C_WIDTH
    n = h_ * chunk

    @pl.when(pl.program_id(1) == 0)
    def _():
        s_ref[...] = s0_ref[...]
        prev_rkv_ref[...] = prkv_ref[...]
        prev_wdad_ref[...] = pwdad_ref[...]

    ri = lax.broadcasted_iota(jnp.int32, (n, n), 0)
    ci = lax.broadcasted_iota(jnp.int32, (n, n), 1)
    same = (ri // chunk) == (ci // chunk)
    row_t, col_t = ri % chunk, ci % chunk
    strict = (same & (row_t > col_t))
    incl = (same & (row_t >= col_t))
    eye = (ri == ci).astype(F32)
    tri = (lax.broadcasted_iota(jnp.int32, (chunk, chunk), 0)
           >= lax.broadcasted_iota(jnp.int32, (chunk, chunk), 1)).astype(BF16)
    seg = ((lax.broadcasted_iota(jnp.int32, (w_, w_), 0) // d_)
           == (lax.broadcasted_iota(jnp.int32, (w_, w_), 1) // d_)).astype(BF16)
    first_row = lax.broadcasted_iota(jnp.int32, (chunk, 1), 0) == 0

    def bd(x):
        return _stack_heads_masked(x, h_, d_).astype(BF16)

    def head_sum(x):
        hi, lo = _split(x)
        return (jnp.dot(hi, seg, preferred_element_type=F32)
                + jnp.dot(lo, seg, preferred_element_type=F32))

    def cumsum_rows(x):
        hi = x.astype(BF16)
        r1 = x - hi.astype(F32)
        mid = r1.astype(BF16)
        lo = (r1 - mid.astype(F32)).astype(BF16)
        return (jnp.dot(tri, hi, preferred_element_type=F32)
                + (jnp.dot(tri, mid, preferred_element_type=F32)
                   + jnp.dot(tri, lo, preferred_element_type=F32)))

    def tn(a, b):
        if a.shape[0] < LANES:
            pad = LANES - a.shape[0]
            a = jnp.concatenate([a, jnp.zeros((pad, a.shape[1]), F32)], axis=0)
            b = jnp.concatenate([b, jnp.zeros((pad, b.shape[1]), BF16)], axis=0)
        return _bdot(a.T, b)

    seqs = list(range(n_seq))
    p_rkv = [rkv_ref[i] for i in seqs]
    p_wdad = [wdad_ref[i] for i in seqs]
    m_rkv = [p + mu_rkv_ref[...] * (jnp.where(first_row, prev_rkv_ref[i], pltpu.roll(p, 1, 0)) - p)
             for i, p in zip(seqs, p_rkv)]
    m_wdad = [p + mu_wdad_ref[...] * (jnp.where(first_row, prev_wdad_ref[i], pltpu.roll(p, 1, 0)) - p)
              for i, p in zip(seqs, p_wdad)]
    r = [x[:, :w_] for x in m_rkv]
    k = [x[:, w_:2 * w_] for x in m_rkv]
    v = [x[:, 2 * w_:] for x in m_rkv]

    def log_decay(x):
        z = -(w0_ref[...] + _dot(jnp.tanh(x), w2_ref[...]))
        softplus = jnp.maximum(z, 0.0) + jnp.log(1.0 + jnp.exp(-jnp.abs(z)))
        return -jnp.exp(-softplus - 0.5)

    lw = _each(log_decay, m_wdad)
    a = _each(lambda x: _sigmoid(a0_ref[...] + _dot(x, a2_ref[...])), m_wdad)
    kk = _each(lambda x: x * kk_ref[...], k)
    kk = _each(lambda x: x / jnp.maximum(jnp.sqrt(head_sum(x * x)), 1e-12), kk)
    k_h = _each(lambda x, y: x * (1.0 + (y - 1.0) * ka_ref[...]), k, a)
    b = _each(lambda x, y: x * y, kk, a)
    bonus = _each(lambda x, y, z: head_sum(x * y * rk_ref[...]) * z, r, k_h, v)

    g = _each(cumsum_rows, lw)
    g_end = [x[chunk - 1:chunk, :] for x in g]
    k_til = _each(lambda x, y, z: bd(x * jnp.exp(y - z)), kk, g, lw)
    r_til = _each(lambda x, y: bd(x * jnp.exp(y)), r, g)
    inv_g = _each(lambda x: jnp.exp(-x), g)
    k_hat = _each(lambda x, y: bd(x * y), k_h, inv_g)
    b_hat = _each(lambda x, y: bd(x * y), b, inv_g)
    to_end = _each(lambda x, y: jnp.exp(x - y), g_end, g)
    k_bar = _each(lambda x, y: bd(x * y), k_h, to_end)
    b_bar = _each(lambda x, y: bd(x * y), b, to_end)
    v_f32 = _each(lambda x: _stack_heads_masked(x, h_, d_), v)
    v_bd = _each(lambda x: x.astype(BF16), v_f32)
    s_f32 = [s_ref[i] for i in seqs]
    s_bd = _each(lambda x: x.astype(BF16), s_f32)

    a_b = _each(lambda x, y: jnp.where(strict, _bdot(x, y, _NT), 0.0), k_til, b_hat)
    a_k = _each(lambda x, y: jnp.where(strict, _bdot(x, y, _NT), 0.0), k_til, k_hat)
    a_rk = _each(lambda x, y: jnp.where(incl, _bdot(x, y, _NT), 0.0), r_til, k_hat)
    a_rb = _each(lambda x, y: jnp.where(incl, _bdot(x, y, _NT), 0.0), r_til, b_hat)
    t_inv = _unit_lower_inverse(a_b, eye, chunk, row_t, col_t)
    rhs = _each(lambda x, y, z, w: _bdot(x, y, _NT) + _bdot(z, w), k_til, s_bd, a_k, v_bd)
    u_f32 = _each(_bdot, t_inv, rhs)
    u = _each(lambda x: x.astype(BF16), u_f32)
    o_bd = _each(lambda x, y, z, w, p, q: _bdot(x, y, _NT) + _bdot(z, w) - _bdot(p, q),
                 r_til, s_bd, a_rk, v_bd, a_rb, u)
    s_new = _each(lambda x, y, vf, kb, uf, bb: x * jnp.exp(y) + tn(vf, kb) - tn(uf, bb),
                  s_f32, g_end, v_f32, k_bar, u_f32, b_bar)

    def group_norm(o):
        out = o[0:chunk]
        for h in range(1, h_):
            out = out + o[h * chunk:(h + 1) * chunk]
        cen = out - head_sum(out) * (1.0 / d_)
        var = head_sum(cen * cen) * (1.0 / d_)
        return cen * lax.rsqrt(var + GN_EPS) * gng_ref[...] + gnb_ref[...]

    out = _each(group_norm, o_bd)
    for i in seqs:
        s_ref[i] = s_new[i]
        o_ref[i] = out[i] + bonus[i]
        prev_rkv_ref[i] = p_rkv[i][chunk - 1:chunk, :]
        prev_wdad_ref[i] = p_wdad[i][chunk - 1:chunk, :]


def _rwkv(rkv, wdad, prev_rkv, prev_wdad, s_bd, params):
    b, t, _ = rkv.shape
    chunk = min(RWKV_CHUNK, t)
    n_seq = next(n for n in (RWKV_SEQS_PER_STEP, 2, 1) if b % n == 0)
    seq = lambda w: pl.BlockSpec((n_seq, chunk, w), lambda i, c: (i, c, 0))
    one = lambda w: pl.BlockSpec((n_seq, 1, w), lambda i, c: (i, 0, 0))
    full = lambda a: pl.BlockSpec(a.shape, lambda i, c: (0,) * a.ndim)
    st = pl.BlockSpec((n_seq, C_WIDTH, C_WIDTH), lambda i, c: (i, 0, 0))
    return pl.pallas_call(
        functools.partial(_rwkv_kernel, chunk=chunk, n_seq=n_seq),
        grid=(b // n_seq, t // chunk),
        in_specs=[seq(3 * C_WIDTH), seq(LANES), one(3 * C_WIDTH), one(LANES), st]
                 + [full(p) for p in params],
        out_specs=[seq(C_WIDTH), st],
        out_shape=[jax.ShapeDtypeStruct((b, t, C_WIDTH), F32),
                   jax.ShapeDtypeStruct((b, C_WIDTH, C_WIDTH), F32)],
        scratch_shapes=[pltpu.VMEM((n_seq, 1, 3 * C_WIDTH), F32), pltpu.VMEM((n_seq, 1, LANES), F32)],
        compiler_params=_cp(("arbitrary", "arbitrary")),
        name="rwkv7",
    )(rkv, wdad, prev_rkv, prev_wdad, s_bd, *params)


def _state_to_bd(state):
    b = state.shape[0]
    z = jnp.zeros((b, C_HEAD_DIM, C_HEAD_DIM), state.dtype)
    rows = [jnp.concatenate([state[:, h] if g == h else z for g in range(C_HEADS)], axis=2)
            for h in range(C_HEADS)]
    return jnp.concatenate(rows, axis=1)


def _state_from_bd(s_bd):
    d = C_HEAD_DIM
    return jnp.stack([s_bd[:, h * d:(h + 1) * d, h * d:(h + 1) * d] for h in range(C_HEADS)], axis=1)


def _out_kernel(x_ref, gate_ref, oa_ref, ga_ref, ob_ref, gb_ref, oc_ref, gc_ref, w_ref, lng_ref,
                lnb_ref, y_ref, *, alpha):
    mixed = (jnp.dot((oa_ref[...] * _silu(ga_ref[...])).astype(BF16), w_ref[0:A_WIDTH, :],
                     preferred_element_type=F32)
             + jnp.dot((ob_ref[...] * _silu(gb_ref[...])).astype(BF16),
                       w_ref[A_WIDTH:A_WIDTH + B_WIDTH, :], preferred_element_type=F32)
             + jnp.dot((oc_ref[...] * _silu(gc_ref[...])).astype(BF16),
                       w_ref[A_WIDTH + B_WIDTH:, :], preferred_element_type=F32))
    z = alpha * x_ref[...] + gate_ref[...] * mixed
    mu = jnp.mean(z, axis=-1, keepdims=True)
    zc = z - mu
    var = jnp.mean(zc * zc, axis=-1, keepdims=True)
    y_ref[...] = zc * lax.rsqrt(var + LN_EPS) * lng_ref[...] + lnb_ref[...]


def _out_proj(x2d, gate, tiles_per_mod, oa, ga, ob, gb, oc, gc, w_out, ln_g, ln_b, alpha, tm):
    r = x2d.shape[0]
    rb = gate.shape[1]
    row = lambda wd: pl.BlockSpec((tm, wd), lambda i: (i, 0))
    full = lambda a: pl.BlockSpec(a.shape, lambda i: (0,) * a.ndim)
    mod = pl.BlockSpec((None, rb, D_MODEL), lambda i: (i // tiles_per_mod, 0, 0))
    return pl.pallas_call(
        functools.partial(_out_kernel, alpha=alpha),
        grid=(r // tm,),
        in_specs=[row(D_MODEL), mod, row(A_WIDTH), row(A_WIDTH), row(B_WIDTH), row(B_WIDTH),
                  row(C_WIDTH), row(C_WIDTH), full(w_out), full(ln_g), full(ln_b)],
        out_specs=row(D_MODEL),
        out_shape=jax.ShapeDtypeStruct((r, D_MODEL), F32),
        compiler_params=_cp(("arbitrary",)),
        name="out_proj",
    )(x2d, gate, oa, ga, ob, gb, oc, gc, w_out, ln_g, ln_b)


def _page_copies(pt_ref, seq, first_page, n_pages, pools, bufs, sems, layer, slot):
    copies = []
    for j in range(n_pages):
        page = pt_ref[seq, first_page + j]
        for pool, buf, sem in zip(pools, bufs, sems):
            copies.append(pltpu.make_async_copy(pool.at[layer, page], buf.at[slot, j], sem.at[slot]))
    return copies


def _moba_sample_kernel(pt_ref, q_ref, kn_ref, vn_ref, expand_ref, pool_kt, pool_vt, o_ref,
                        buf, sem, kst_ref, m_ref, l_ref, acc_ref, *, pg, ng, n_blk, layer):
    b = pl.program_id(0)
    t = q_ref.shape[0]
    rows = A_HEADS * t
    per = MOBA_BLOCK // PAGE_SIZE
    bpg = pg // per

    def copies(seq, i):
        pool = pool_kt if i < ng else pool_vt
        return _page_copies(pt_ref, seq, (i % ng) * pg, pg, (pool,), (buf,), (sem,), layer, i % 2)

    def start(seq, i):
        for cp in copies(seq, i):
            cp.start()

    def wait(seq, i):
        for cp in copies(seq, i):
            cp.wait()

    @pl.when(b == 0)
    def _():
        start(b, 0)

    qm = _stack_heads_masked(q_ref[...], A_HEADS, A_HEAD_DIM)
    qb = (qm * (A_HEAD_DIM ** -0.5)).astype(BF16)

    lane = lax.broadcasted_iota(jnp.int32, (A_WIDTH, LANES), 1)
    kmean = jnp.zeros((A_WIDTH, LANES), F32)
    for g in range(ng):
        start(b, g + 1)
        wait(b, g)
        for i in range(bpg):
            blk_sum = buf[g % 2, i * per]
            for j in range(1, per):
                blk_sum = blk_sum + buf[g % 2, i * per + j]
            col = jnp.sum(blk_sum, axis=1, keepdims=True) * (1.0 / MOBA_BLOCK)
            kmean = jnp.where(lane == g * bpg + i, col, kmean)
        kst_ref[g] = jnp.concatenate([buf[g % 2, j].astype(BF16) for j in range(pg)], axis=1)

    gate = _doth(qm, kmean)
    blk = lax.broadcasted_iota(jnp.int32, gate.shape, 1)
    bias = jnp.where(_top_select(gate, blk < n_blk, 1), 0.0, NEG)

    _softmax_reset(m_ref, l_ref, acc_ref)
    s = _dot_nt(qb, kn_ref[...])
    qt = lax.broadcasted_iota(jnp.int32, (rows, t), 0) % t
    kt = lax.broadcasted_iota(jnp.int32, (rows, t), 1)
    _softmax_update(jnp.where(kt <= qt, s, NEG), m_ref, l_ref, acc_ref, vn_ref[...].astype(BF16), rows)

    r_i = lax.broadcasted_iota(jnp.int32, (LANES, LANES), 0)
    c_i = lax.broadcasted_iota(jnp.int32, (LANES, LANES), 1)
    for g in range(ng):
        if g + 1 < ng:
            start(b, ng + g + 1)
        else:
            @pl.when(b + 1 < pl.num_programs(0))
            def _():
                start(b + 1, 0)
        wait(b, ng + g)
        vc = jnp.concatenate([buf[(ng + g) % 2, j].astype(BF16) for j in range(pg)], axis=1)
        pick = ((r_i == g * bpg + c_i) & (c_i < bpg)).astype(BF16)
        bias_g = _dot(bias, pick)[:, :bpg].astype(BF16)
        s = (jnp.dot(qb, kst_ref[g], preferred_element_type=F32)
             + jnp.dot(bias_g, expand_ref[...], preferred_element_type=F32))
        _softmax_update(s, m_ref, l_ref, acc_ref, vc, rows, nt=True)

    o_ref[...] = _merge_heads(acc_ref[...] / l_ref[...], A_HEADS, A_HEAD_DIM)


def _moba_sample(q, k_new, v_new, pool_kt, pool_vt, page_table, layer, pg):
    db, t, w = q.shape
    n_pages = page_table.shape[1]
    n_blk = n_pages * PAGE_SIZE // MOBA_BLOCK
    assert n_blk <= LANES
    ng = n_pages // pg
    bpg = pg * PAGE_SIZE // MOBA_BLOCK
    rows = A_HEADS * t
    tok = pl.BlockSpec((None, t, w), lambda b, pt: (b, 0, 0))
    hbm = pl.BlockSpec(memory_space=pl.ANY)
    key_blk = jnp.arange(pg * PAGE_SIZE, dtype=jnp.int32) // MOBA_BLOCK
    expand = (jnp.arange(bpg, dtype=jnp.int32)[:, None] == key_blk[None, :]).astype(BF16)
    return pl.pallas_call(
        functools.partial(_moba_sample_kernel, pg=pg, ng=ng, n_blk=n_blk, layer=layer),
        grid_spec=pltpu.PrefetchScalarGridSpec(
            num_scalar_prefetch=1, grid=(db,),
            in_specs=[tok, tok, tok, pl.BlockSpec(expand.shape, lambda b, pt: (0, 0)), hbm, hbm],
            out_specs=tok,
            scratch_shapes=[pltpu.VMEM((2, pg, w, PAGE_SIZE), F32), pltpu.SemaphoreType.DMA((2,)),
                            pltpu.VMEM((ng, w, pg * PAGE_SIZE), BF16),
                            pltpu.VMEM((rows, 1), F32), pltpu.VMEM((rows, 1), F32),
                            pltpu.VMEM((rows, w), F32)]),
        out_shape=jax.ShapeDtypeStruct((db, t, w), F32),
        compiler_params=_cp(("arbitrary",)),
        name="moba_sample",
    )(page_table, q, k_new, v_new, expand, pool_kt, pool_vt)


def _mla_sample_kernel(pt_ref, qabs_ref, qrope_ref, cn_ref, rn_ref, wuv_ref, pool_c, pool_rt, o_ref,
                       cbuf, rbuf, csem, rsem, m_ref, l_ref, acc_ref, *, pg, layer):
    b, g = pl.program_id(0), pl.program_id(1)
    nb, ng = pl.num_programs(0), pl.num_programs(1)
    step = b * ng + g
    slot = step % 2
    t = qabs_ref.shape[0]
    rows = B_HEADS * t

    def group_copies(step_idx, slot_idx):
        return _page_copies(pt_ref, step_idx // ng, (step_idx % ng) * pg, pg, (pool_c, pool_rt),
                            (cbuf, rbuf), (csem, rsem), layer, slot_idx)

    @pl.when(step == 0)
    def _():
        for cp in group_copies(step, slot):
            cp.start()

    @pl.when(step + 1 < nb * ng)
    def _():
        for cp in group_copies(step + 1, 1 - slot):
            cp.start()

    qs, qr = _stack_mla_q(qabs_ref, qrope_ref)

    @pl.when(g == 0)
    def _():
        _softmax_reset(m_ref, l_ref, acc_ref)
        cn = cn_ref[...].astype(BF16)
        s = _dot_nt(qs, cn) + _dot_nt(qr, rn_ref[...])
        qt = lax.broadcasted_iota(jnp.int32, (rows, t), 0) % t
        kt = lax.broadcasted_iota(jnp.int32, (rows, t), 1)
        _softmax_update(jnp.where(kt <= qt, s, NEG), m_ref, l_ref, acc_ref, cn, rows)

    for cp in group_copies(step, slot):
        cp.wait()
    cc = cbuf[slot].reshape(pg * PAGE_SIZE, B_KV_RANK).astype(BF16)
    rc = jnp.concatenate([rbuf[slot, j].astype(BF16) for j in range(pg)], axis=1)
    s = _dot_nt(qs, cc) + jnp.dot(qr, rc, preferred_element_type=F32)
    _softmax_update(s, m_ref, l_ref, acc_ref, cc, rows)

    @pl.when(g == ng - 1)
    def _():
        _mla_finish(acc_ref[...], l_ref[...], wuv_ref, o_ref, t)


def _mla_sample(qabs, qrope, c_new, r_new, pool_c, pool_rt, page_table, wuv, layer, pg):
    db, t, _ = qabs.shape
    n_pages = page_table.shape[1]
    rows = B_HEADS * t
    tok = lambda w: pl.BlockSpec((None, t, w), lambda b, g, pt: (b, 0, 0))
    hbm = pl.BlockSpec(memory_space=pl.ANY)
    return pl.pallas_call(
        functools.partial(_mla_sample_kernel, pg=pg, layer=layer),
        grid_spec=pltpu.PrefetchScalarGridSpec(
            num_scalar_prefetch=1, grid=(db, n_pages // pg),
            in_specs=[tok(qabs.shape[2]), tok(qrope.shape[2]), tok(B_KV_RANK), tok(B_ROPE_DIM),
                      pl.BlockSpec(wuv.shape, lambda b, g, pt: (0, 0, 0)), hbm, hbm],
            out_specs=tok(B_WIDTH),
            scratch_shapes=[pltpu.VMEM((2, pg, PAGE_SIZE, B_KV_RANK), F32),
                            pltpu.VMEM((2, pg, B_ROPE_DIM, PAGE_SIZE), F32),
                            pltpu.SemaphoreType.DMA((2,)), pltpu.SemaphoreType.DMA((2,)),
                            pltpu.VMEM((rows, 1), F32), pltpu.VMEM((rows, 1), F32),
                            pltpu.VMEM((rows, B_KV_RANK), F32)]),
        out_shape=jax.ShapeDtypeStruct((db, t, B_WIDTH), F32),
        compiler_params=_cp(("arbitrary", "arbitrary")),
        name="mla_sample",
    )(page_table, qabs, qrope, c_new, r_new, wuv, pool_c, pool_rt)


def _prep_w_in(w):
    z = jnp.zeros((w.shape[0], 64), w.dtype)
    return jnp.concatenate(
        [w[:, 0:1536], w[:, 1600:2112], w[:, 2112:2880], w[:, 3008:3264], w[:, 1536:1600], z,
         w[:, 2880:3008]], axis=1).astype(BF16)


def _prep_w_uq(w):
    w = w.reshape(B_Q_RANK, B_HEADS, B_NOPE_DIM + B_ROPE_DIM)
    nope = w[:, :, :B_NOPE_DIM].reshape(B_Q_RANK, B_HEADS * B_NOPE_DIM)
    rope = jnp.pad(w[:, :, B_NOPE_DIM:], ((0, 0), (0, 0), (0, LANES - B_ROPE_DIM)))
    return jnp.concatenate([nope, rope.reshape(B_Q_RANK, B_HEADS * LANES)], axis=1).astype(BF16)


def _pad_rows_64(w, top):
    z = jnp.zeros((64, w.shape[1]), w.dtype)
    return jnp.concatenate([w, z] if top else [z, w], axis=0)


def kernel(x_prompt, x_sample, cache_moba_k, cache_moba_v, cache_mla_latent, cache_mla_rope, state_rwkv_wkv, state_rwkv_shift, page_table, c_prompt, c_sample, w_ada, b_ada, w_in, mla_q_norm, mla_kv_norm, mla_w_uq, mla_w_uk, mla_w_uv, rwkv_mu, rwkv_w0, rwkv_w2, rwkv_a0, rwkv_a2, rwkv_k_k, rwkv_k_a, rwkv_r_k, rwkv_gn_g, rwkv_gn_b, w_out, ln_g, ln_b):
    depth = w_in.shape[0]
    bp, sp, d = x_prompt.shape
    db, ts, _ = x_sample.shape
    n_pages = page_table.shape[1]
    n_pool = cache_moba_k.shape[1]
    past = n_pages * PAGE_SIZE
    alpha = (2 * depth) ** 0.25
    tm_p = min(256, sp)
    rs = db * ts
    tm_s = rs
    pg = min(32, n_pages)

    mods = _ada_all(jnp.concatenate([c_prompt, c_sample], axis=0), w_ada, b_ada)

    pos_p = jnp.arange(sp, dtype=jnp.int32)
    pos_s = jnp.tile(past + jnp.arange(ts, dtype=jnp.int32), db)
    tabs = [(_rope_tables(p, A_ROPE_DIM // 2, ROPE_THETA, A_HEAD_DIM, A_HEADS),
             _rope_tables(p, B_ROPE_DIM // 2, MLA_ROPE_THETA, LANES, 1)) for p in (pos_p, pos_s)]

    pool_kt = jnp.transpose(cache_moba_k, (0, 1, 3, 4, 2)).reshape(depth, n_pool, A_WIDTH, PAGE_SIZE)
    pool_vt = jnp.transpose(cache_moba_v, (0, 1, 3, 4, 2)).reshape(depth, n_pool, A_WIDTH, PAGE_SIZE)
    pool_rt = jnp.transpose(cache_mla_rope, (0, 1, 3, 2))

    xp = x_prompt.reshape(bp * sp, d)
    xs = x_sample.reshape(rs, d)
    zero_state = jnp.zeros((bp, C_WIDTH, C_WIDTH), F32)
    outs = [[] for _ in range(12)]

    for l in range(depth):
        w = _prep_w_in(w_in[l])
        wuq = _prep_w_uq(mla_w_uq[l])
        wuk = jnp.transpose(mla_w_uk[l], (1, 2, 0)).astype(BF16)
        wuv = jnp.transpose(mla_w_uv[l], (1, 0, 2)).astype(BF16)
        qn = mla_q_norm[l].reshape(1, -1)
        kvn = mla_kv_norm[l].reshape(1, -1)
        mu = rwkv_mu[l]
        row = lambda a: a.reshape(1, -1)
        rwkv_params = (row(mu[:3 * C_WIDTH]), row(mu[3 * C_WIDTH:]), row(rwkv_w0[l]),
                       _pad_rows_64(rwkv_w2[l], True).astype(BF16), row(rwkv_a0[l]),
                       _pad_rows_64(rwkv_a2[l], False).astype(BF16), row(rwkv_k_k[l]),
                       row(rwkv_k_a[l]), row(rwkv_r_k[l]), row(rwkv_gn_g[l]), row(rwkv_gn_b[l]))
        wo = w_out[l].astype(BF16)
        lng, lnb = row(ln_g[l]), row(ln_b[l])

        for stream in (0, 1):
            if stream == 0:
                x2d, nb_, t_, tm = xp, bp, sp, tm_p
                mod = mods[l, :bp].reshape(bp, 1, 3 * d)
                tiles_per_mod = sp // tm
                prev = jnp.zeros((bp, 3 * C_WIDTH + LANES), F32)
                s0 = zero_state
            else:
                x2d, nb_, t_, tm = xs, db, ts, tm_s
                mod = jnp.repeat(mods[l, bp:], ts, axis=0).reshape(1, rs, 3 * d)
                tiles_per_mod = 1
                prev = state_rwkv_shift[l]
                s0 = _state_to_bd(state_rwkv_wkv[l])
            shift, scale, gate = mod[..., :d], mod[..., d:2 * d], mod[..., 2 * d:]
            tab_a, tab_b = tabs[stream]
            (qa, ka, va, ga, qabs, qrope, ckv, kr, gb, rkv, wdad, gc) = _in_proj(
                x2d, scale, shift, tiles_per_mod, w, wuq, wuk, qn, kvn, tab_a, tab_b, tm)
            r3 = lambda a: a.reshape(nb_, t_, a.shape[-1])
            if stream == 0:
                oa = _moba_prompt(r3(qa), r3(ka), r3(va))
                ob = _mla_prompt(r3(qabs), r3(qrope), r3(ckv), r3(kr), wuv)
            else:
                oa = _moba_sample(r3(qa), r3(ka), r3(va), pool_kt, pool_vt, page_table, l, pg)
                ob = _mla_sample(r3(qabs), r3(qrope), r3(ckv), r3(kr), cache_mla_latent,
                                 pool_rt, page_table, wuv, l, pg)
            oc, s_new = _rwkv(r3(rkv), r3(wdad), prev[:, None, :3 * C_WIDTH],
                              prev[:, None, 3 * C_WIDTH:], s0, rwkv_params)
            y = _out_proj(x2d, gate, tiles_per_mod, oa.reshape(-1, A_WIDTH), ga,
                          ob.reshape(-1, B_WIDTH), gb, oc.reshape(-1, C_WIDTH), gc, wo, lng, lnb,
                          alpha, tm)
            last = jnp.concatenate([r3(rkv)[:, -1], r3(wdad)[:, -1]], axis=-1)
            leaves = (ka.reshape(nb_, t_, A_HEADS, A_HEAD_DIM), va.reshape(nb_, t_, A_HEADS, A_HEAD_DIM),
                      r3(ckv), r3(kr), _state_from_bd(s_new), last)
            for i, leaf in enumerate(leaves):
                outs[stream * 6 + i].append(leaf)
            if stream == 0:
                xp = y
            else:
                xs = y

    return (xp.reshape(bp, sp, d), xs.reshape(db, ts, d)) + tuple(jnp.stack(o) for o in outs)
```

```python
import functools
import math

import jax
import jax.numpy as jnp
from jax import lax
from jax.experimental import pallas as pl
from jax.experimental.pallas import tpu as pltpu

F32 = jnp.float32
BF16 = jnp.bfloat16
HI = lax.Precision.HIGHEST

D_MODEL = 1024
PAGE_SIZE = 128
LANES = 128
A_HEADS = 4
A_HEAD_DIM = 64
A_WIDTH = 256
A_ROPE_DIM = 16
ROPE_THETA = 500000.0
MOBA_BLOCK = 256
MOBA_TOPK = 3
Q_BLOCK = 256
B_HEADS = 4
B_NOPE_DIM = 128
B_ROPE_DIM = 64
B_V_DIM = 128
B_WIDTH = 512
B_Q_RANK = 256
B_KV_RANK = 256
MLA_ROPE_THETA = 10000.0
MLA_SCALE = (B_NOPE_DIM + B_ROPE_DIM) ** -0.5
C_HEADS = 4
C_HEAD_DIM = 64
C_WIDTH = 256
C_DECAY_RANK = 64
C_A_RANK = 64
GN_EPS = 64e-5
RWKV_CHUNK = 64
RWKV_SUB = 16
RWKV_SEQS_PER_STEP = 8
LN_EPS = 1e-5
RMS_EPS = 1e-6
NEG = -1e30

OFF_QA, OFF_KA, OFF_VA, OFF_GA = 0, 256, 512, 768
OFF_CQ, OFF_CKV, OFF_GB, OFF_RKV, OFF_GC, OFF_KR, OFF_WDAD = 1024, 1280, 1536, 2048, 2816, 3072, 3200
IN_COLS_PAD = 3328

VMEM_LIMIT = 56 * 1024 * 1024


def _cp(sem):
    return pltpu.CompilerParams(dimension_semantics=sem, vmem_limit_bytes=VMEM_LIMIT)


def _dot(a, b):
    return jnp.dot(a.astype(BF16), b.astype(BF16), preferred_element_type=F32)


def _dot_nt(a, b):
    return lax.dot_general(a.astype(BF16), b.astype(BF16), (((1,), (1,)), ((), ())),
                           preferred_element_type=F32)


def _doth(a, b):
    return jnp.dot(a, b, precision=HI, preferred_element_type=F32)


def _doth_nt(a, b):
    return lax.dot_general(a, b, (((1,), (1,)), ((), ())), precision=HI, preferred_element_type=F32)


_NN = (((1,), (0,)), ((), ()))
_NT = (((1,), (1,)), ((), ()))


def _split(x):
    hi = x.astype(BF16)
    return hi, (x - hi.astype(F32)).astype(BF16)


def _bdot(a, b, dims=_NN):
    return lax.dot_general(a.astype(BF16), b.astype(BF16), dims, preferred_element_type=F32)


def _sigmoid(x):
    return 1.0 / (1.0 + jnp.exp(-x))


def _silu(x):
    return x * _sigmoid(x)


def _head_mask(rows, width, head, head_dim):
    lane = lax.broadcasted_iota(jnp.int32, (rows, width), 1)
    return (lane >= head * head_dim) & (lane < (head + 1) * head_dim)


def _ada_kernel(c_ref, w_ref, b_ref, o_ref):
    o_ref[...] = _dot(_silu(c_ref[...]), w_ref[...]) + b_ref[...]


def _ada_all(c_all, w_ada, b_ada):
    depth = w_ada.shape[0]
    r = c_all.shape[0]
    nj = w_ada.shape[2] // D_MODEL
    return pl.pallas_call(
        _ada_kernel,
        grid=(depth, nj),
        in_specs=[pl.BlockSpec((r, D_MODEL), lambda l, j: (0, 0)),
                  pl.BlockSpec((None, D_MODEL, D_MODEL), lambda l, j: (l, 0, j)),
                  pl.BlockSpec((None, 1, D_MODEL), lambda l, j: (l, 0, j))],
        out_specs=pl.BlockSpec((None, r, D_MODEL), lambda l, j: (l, 0, j)),
        out_shape=jax.ShapeDtypeStruct((depth, r, w_ada.shape[2]), F32),
        compiler_params=_cp(("arbitrary", "arbitrary")),
        name="adaln_mod",
    )(c_all, w_ada, b_ada.reshape(depth, 1, -1))


def _rope_tables(pos, rot_half, theta, head_width, n_rep):
    inv = jnp.power(theta, -jnp.arange(rot_half, dtype=F32) / rot_half)
    ang = pos.astype(F32)[:, None] * inv[None, :]
    cos, sin = jnp.cos(ang), jnp.sin(ang)
    p = pos.shape[0]
    rest = head_width - 2 * rot_half
    c = jnp.concatenate([cos, cos, jnp.ones((p, rest), F32)], axis=1)
    s1 = jnp.concatenate([-sin, jnp.zeros((p, head_width - rot_half), F32)], axis=1)
    s2 = jnp.concatenate([jnp.zeros((p, rot_half), F32), sin, jnp.zeros((p, rest), F32)], axis=1)
    return jnp.stack([jnp.tile(t, (1, n_rep)) for t in (c, s1, s2)])


def _apply_rope(x, cos, s1, s2, half):
    w = x.shape[-1]
    return x * cos + pltpu.roll(x, w - half, 1) * s1 + pltpu.roll(x, half, 1) * s2


def _rms(x, g):
    return x * lax.rsqrt(jnp.mean(x * x, axis=-1, keepdims=True) + RMS_EPS) * g


def _in_kernel(x_ref, sc_ref, sh_ref, w_ref, wuq_ref, wuk_ref, qn_ref, kvn_ref, ta_ref, tb_ref,
               qa_ref, ka_ref, va_ref, ga_ref, qabs_ref, qrope_ref, ckv_ref, kr_ref, gb_ref,
               rkv_ref, wdad_ref, gc_ref):
    hb = (x_ref[...] * (1.0 + sc_ref[...]) + sh_ref[...]).astype(BF16)

    def proj(off, width):
        return jnp.dot(hb, w_ref[:, off:off + width], preferred_element_type=F32)

    ca, sa1, sa2 = ta_ref[0], ta_ref[1], ta_ref[2]
    cb, sb1, sb2 = tb_ref[0], tb_ref[1], tb_ref[2]
    qa_ref[...] = _apply_rope(proj(OFF_QA, A_WIDTH), ca, sa1, sa2, A_ROPE_DIM // 2)
    ka_ref[...] = _apply_rope(proj(OFF_KA, A_WIDTH), ca, sa1, sa2, A_ROPE_DIM // 2)
    va_ref[...] = proj(OFF_VA, A_WIDTH)
    ga_ref[...] = proj(OFF_GA, A_WIDTH)
    cqn = _rms(proj(OFF_CQ, B_Q_RANK), qn_ref[...])
    qf = _dot(cqn, wuq_ref[...])
    for h in range(B_HEADS):
        qabs_ref[:, h * B_KV_RANK:(h + 1) * B_KV_RANK] = _dot(
            qf[:, h * B_NOPE_DIM:(h + 1) * B_NOPE_DIM], wuk_ref[h])
        base = B_HEADS * B_NOPE_DIM + h * LANES
        qrope_ref[:, h * LANES:(h + 1) * LANES] = _apply_rope(
            qf[:, base:base + LANES], cb, sb1, sb2, B_ROPE_DIM // 2)
    ckv_ref[...] = _rms(proj(OFF_CKV, B_KV_RANK), kvn_ref[...])
    kr_ref[...] = _apply_rope(proj(OFF_KR, LANES), cb, sb1, sb2, B_ROPE_DIM // 2)[:, :B_ROPE_DIM]
    gb_ref[...] = proj(OFF_GB, B_WIDTH)
    rkv_ref[...] = proj(OFF_RKV, 3 * C_WIDTH)
    wdad_ref[...] = proj(OFF_WDAD, LANES)
    gc_ref[...] = proj(OFF_GC, C_WIDTH)


def _in_proj(x2d, scale, shift, tiles_per_mod, w, wuq, wuk, qn, kvn, tab_a, tab_b, tm):
    r = x2d.shape[0]
    n_tiles = r // tm
    tab_tiles = tab_a.shape[1] // tm
    rb = scale.shape[1]
    widths = [A_WIDTH, A_WIDTH, A_WIDTH, A_WIDTH, B_HEADS * B_KV_RANK, B_HEADS * LANES, B_KV_RANK,
              B_ROPE_DIM, B_WIDTH, 3 * C_WIDTH, LANES, C_WIDTH]
    row = lambda wd: pl.BlockSpec((tm, wd), lambda i: (i, 0))
    full = lambda a: pl.BlockSpec(a.shape, lambda i: (0,) * a.ndim)
    mod = pl.BlockSpec((None, rb, D_MODEL), lambda i: (i // tiles_per_mod, 0, 0))
    tab = lambda a: pl.BlockSpec((3, tm, a.shape[2]), lambda i: (0, i % tab_tiles, 0))
    return pl.pallas_call(
        _in_kernel,
        grid=(n_tiles,),
        in_specs=[row(D_MODEL), mod, mod, full(w), full(wuq), full(wuk), full(qn), full(kvn),
                  tab(tab_a), tab(tab_b)],
        out_specs=[row(wd) for wd in widths],
        out_shape=[jax.ShapeDtypeStruct((r, wd), F32) for wd in widths],
        compiler_params=_cp(("arbitrary",)),
        name="in_proj",
    )(x2d, scale, shift, w, wuq, wuk, qn, kvn, tab_a, tab_b)


def _top_select(g, valid, axis):
    n = g.shape[axis]
    idx = lax.broadcasted_iota(jnp.int32, g.shape, axis).astype(F32)
    g = jnp.where(valid, g, -jnp.inf)
    sel = jnp.zeros(g.shape, jnp.bool_)
    for _ in range(min(MOBA_TOPK, n)):
        mx = jnp.max(g, axis=axis, keepdims=True)
        first = jnp.min(jnp.where(g == mx, idx, float(n)), axis=axis, keepdims=True)
        hit = idx == first
        sel = sel | hit
        g = jnp.where(hit, -jnp.inf, g)
    return sel & valid


def _stack_heads_masked(q, n_heads, head_dim):
    r, w = q.shape
    return jnp.concatenate(
        [jnp.where(_head_mask(r, w, h, head_dim), q, 0.0) for h in range(n_heads)], axis=0)


def _merge_heads(acc, n_heads, head_dim):
    r = acc.shape[0] // n_heads
    w = acc.shape[1]
    out = jnp.zeros((r, w), F32)
    for h in range(n_heads):
        out = out + jnp.where(_head_mask(r, w, h, head_dim), acc[h * r:(h + 1) * r], 0.0)
    return out


def _softmax_reset(m_ref, l_ref, acc_ref):
    m_ref[...] = jnp.full(m_ref.shape, NEG, F32)
    l_ref[...] = jnp.zeros(l_ref.shape, F32)
    acc_ref[...] = jnp.zeros(acc_ref.shape, F32)


def _softmax_update(s, m_ref, l_ref, acc_ref, v, group, nt=False):
    ps = []
    for g0 in range(0, s.shape[0], group):
        sl = slice(g0, g0 + group)
        sg = s[sl]
        m_old = m_ref[sl]
        m_new = jnp.maximum(m_old, jnp.max(sg, axis=-1, keepdims=True))
        alpha = jnp.exp(m_old - m_new)
        p = jnp.exp(sg - m_new)
        l_ref[sl] = alpha * l_ref[sl] + jnp.sum(p, axis=-1, keepdims=True)
        m_ref[sl] = m_new
        acc_ref[sl] = alpha * acc_ref[sl]
        ps.append(p.astype(BF16))
    p_all = ps[0] if len(ps) == 1 else jnp.concatenate(ps, axis=0)
    acc_ref[...] += lax.dot_general(p_all, v, _NT if nt else _NN, preferred_element_type=F32)


def _stack_mla_q(qabs_ref, qrope_ref):
    qs = jnp.concatenate([qabs_ref[:, h * B_KV_RANK:(h + 1) * B_KV_RANK] for h in range(B_HEADS)], axis=0)
    qr = jnp.concatenate([qrope_ref[:, h * LANES:h * LANES + B_ROPE_DIM] for h in range(B_HEADS)], axis=0)
    return (qs * MLA_SCALE).astype(BF16), (qr * MLA_SCALE).astype(BF16)


def _mla_finish(acc, l, wuv_ref, o_ref, r):
    o = acc / l
    for h in range(B_HEADS):
        o_ref[:, h * B_V_DIM:(h + 1) * B_V_DIM] = _dot(o[h * r:(h + 1) * r], wuv_ref[h])


def _prompt_attn_kernel(q_ref, k_ref, v_ref, qabs_ref, qrope_ref, ckv_ref, kr_ref, wuv_ref,
                        oa_ref, ob_ref, kmean_ref, vt_ref, bias_ref, acca_ref, ckvt_ref, accb_ref, *, nb):
    c = pl.program_id(1)
    blk_n = MOBA_BLOCK
    nbp = kmean_ref.shape[0]
    rows = A_HEADS * Q_BLOCK

    @pl.when(c == 0)
    def _():
        means = [jnp.mean(k_ref[n * blk_n:(n + 1) * blk_n, :], axis=0, keepdims=True) for n in range(nb)]
        pad = [jnp.zeros((nbp - nb, A_WIDTH), F32)] if nbp > nb else []
        kmean_ref[...] = jnp.concatenate(means + pad, axis=0)
        for n in range(nb):
            vt_ref[n] = v_ref[n * blk_n:(n + 1) * blk_n, :].T.astype(BF16)
            ckvt_ref[n] = ckv_ref[n * blk_n:(n + 1) * blk_n, :].T.astype(BF16)

    qm = _stack_heads_masked(q_ref[...], A_HEADS, A_HEAD_DIM)
    gate_t = _doth_nt(kmean_ref[...], qm)
    blk = lax.broadcasted_iota(jnp.int32, gate_t.shape, 0)
    bias_ref[...] = jnp.where(_top_select(gate_t, blk < c, 0), 0.0, NEG)
    qb = (qm * (A_HEAD_DIM ** -0.5)).astype(BF16)
    qs, qr = _stack_mla_q(qabs_ref, qrope_ref)
    acca_ref[...] = jnp.zeros(acca_ref.shape, F32)
    accb_ref[...] = jnp.zeros(accb_ref.shape, F32)
    qpos = c * Q_BLOCK + lax.broadcasted_iota(jnp.int32, (1, rows), 1) % Q_BLOCK

    def block(n, carry, own_block):
        ma, la, mb, lb = carry
        start = pl.multiple_of(n * blk_n, blk_n)
        sa = _dot_nt(k_ref[pl.ds(start, blk_n), :], qb)
        sb = (_dot_nt(ckv_ref[pl.ds(start, blk_n), :], qs)
              + _dot_nt(kr_ref[pl.ds(start, blk_n), :], qr))
        if own_block:
            allowed = (n * blk_n + lax.broadcasted_iota(jnp.int32, (blk_n, 1), 0)) <= qpos
            sa = jnp.where(allowed, sa, NEG)
            sb = jnp.where(allowed, sb, NEG)
        else:
            sa = sa + bias_ref[pl.ds(n, 1), :]
        ma_new = jnp.maximum(ma, jnp.max(sa, axis=0, keepdims=True))
        mb_new = jnp.maximum(mb, jnp.max(sb, axis=0, keepdims=True))
        alpha_a = jnp.exp(ma - ma_new)
        alpha_b = jnp.exp(mb - mb_new)
        pa = jnp.exp(sa - ma_new)
        pb = jnp.exp(sb - mb_new)
        la = alpha_a * la + jnp.sum(pa, axis=0, keepdims=True)
        lb = alpha_b * lb + jnp.sum(pb, axis=0, keepdims=True)
        pa = pa.astype(BF16)
        for h in range(A_HEADS):
            cols = slice(h * Q_BLOCK, (h + 1) * Q_BLOCK)
            acca_ref[h] = acca_ref[h] * alpha_a[:, cols] + jnp.dot(
                vt_ref[n, h * A_HEAD_DIM:(h + 1) * A_HEAD_DIM, :], pa[:, cols],
                preferred_element_type=F32)
        accb_ref[...] = accb_ref[...] * alpha_b + jnp.dot(ckvt_ref[n], pb.astype(BF16),
                                                          preferred_element_type=F32)
        return ma_new, la, mb_new, lb

    neg = jnp.full((1, rows), NEG, F32)
    zero = jnp.zeros((1, rows), F32)
    carry = block(c, (neg, zero, neg, zero), True)
    ma, la, mb, lb = lax.fori_loop(0, c, lambda n, cr: block(n, cr, False), carry)

    inv_la = 1.0 / la
    merged_t = jnp.concatenate(
        [acca_ref[h] * inv_la[:, h * Q_BLOCK:(h + 1) * Q_BLOCK] for h in range(A_HEADS)], axis=0)
    oa_ref[...] = merged_t.T
    o_t = accb_ref[...] * (1.0 / lb)
    for h in range(B_HEADS):
        o_h = o_t[:, h * Q_BLOCK:(h + 1) * Q_BLOCK].T
        ob_ref[:, h * B_V_DIM:(h + 1) * B_V_DIM] = _dot(o_h, wuv_ref[h])


def _prompt_attn(q, k, v, qabs, qrope, ckv, kr, wuv):
    b, s, w = q.shape
    assert Q_BLOCK == MOBA_BLOCK and s % MOBA_BLOCK == 0
    nb = s // MOBA_BLOCK
    nbp = -(-nb // 8) * 8
    rows = A_HEADS * Q_BLOCK
    blk = lambda wd: pl.BlockSpec((None, Q_BLOCK, wd), lambda i, c: (i, c, 0))
    seq = lambda wd: pl.BlockSpec((None, s, wd), lambda i, c: (i, 0, 0))
    return pl.pallas_call(
        functools.partial(_prompt_attn_kernel, nb=nb),
        grid=(b, nb),
        in_specs=[blk(w), seq(w), seq(w), blk(qabs.shape[2]), blk(qrope.shape[2]), seq(B_KV_RANK),
                  seq(B_ROPE_DIM), pl.BlockSpec(wuv.shape, lambda i, c: (0, 0, 0))],
        out_specs=[blk(w), blk(B_WIDTH)],
        out_shape=[jax.ShapeDtypeStruct((b, s, w), F32), jax.ShapeDtypeStruct((b, s, B_WIDTH), F32)],
        scratch_shapes=[pltpu.VMEM((nbp, w), F32), pltpu.VMEM((nb, w, MOBA_BLOCK), BF16),
                        pltpu.VMEM((nbp, rows), F32), pltpu.VMEM((A_HEADS, A_HEAD_DIM, Q_BLOCK), F32),
                        pltpu.VMEM((nb, B_KV_RANK, MOBA_BLOCK), BF16),
                        pltpu.VMEM((B_KV_RANK, rows), F32)],
        compiler_params=_cp(("arbitrary", "arbitrary")),
        name="prompt_attn",
    )(q, k, v, qabs, qrope, ckv, kr, wuv)


def _each(f, *lists):
    return [f(*xs) for xs in zip(*lists)]


def _neumann(a, eye, levels):
    t = _each(lambda x: eye - x, a)
    p = _each(lambda x: x.astype(BF16), a)
    for _ in range(levels - 1):
        p = _each(lambda x: _bdot(x, x).astype(BF16), p)
        t = _each(lambda y, x: y + _bdot(y, x), t, p)
    return t


def _unit_lower_inverse(a, eye, chunk, row_t, col_t):
    if chunk <= RWKV_SUB:
        return _neumann(a, eye, max(1, math.ceil(math.log2(chunk))))
    near = (row_t // RWKV_SUB) == (col_t // RWKV_SUB)
    a_d = _each(lambda x: jnp.where(near, x, 0.0), a)
    t_d = _each(lambda x: x.astype(BF16), _neumann(a_d, eye, int(math.log2(RWKV_SUB))))
    n = _each(lambda t, x, xd: _bdot(t, x - xd), t_d, a, a_d)
    t_n = _neumann(n, eye, max(1, math.ceil(math.log2(chunk // RWKV_SUB))))
    return _each(_bdot, t_n, t_d)


def _rwkv_kernel(rkv_ref, wdad_ref, prkv_ref, pwdad_ref, s0_ref, mu_rkv_ref, mu_wdad_ref,
                 w0_ref, w2_ref, a0_ref, a2_ref, kk_ref, ka_ref, rk_ref, gng_ref, gnb_ref,
                 o_ref, s_ref, prev_rkv_ref, prev_wdad_ref, *, chunk, n_seq):
    h_, d_, w_ = C_HEADS, C_HEAD_DIM, C_WIDTH
    n = h_ * chunk

    @pl.when(pl.program_id(1) == 0)
    def _():
        s_ref[...] = s0_ref[...]
        prev_rkv_ref[...] = prkv_ref[...]
        prev_wdad_ref[...] = pwdad_ref[...]

    ri = lax.broadcasted_iota(jnp.int32, (n, n), 0)
    ci = lax.broadcasted_iota(jnp.int32, (n, n), 1)
    same = (ri // chunk) == (ci // chunk)
    row_t, col_t = ri % chunk, ci % chunk
    strict = (same & (row_t > col_t))
    incl = (same & (row_t >= col_t))
    eye = (ri == ci).astype(F32)
    tri = (lax.broadcasted_iota(jnp.int32, (chunk, chunk), 0)
           >= lax.broadcasted_iota(jnp.int32, (chunk, chunk), 1)).astype(BF16)
    seg = ((lax.broadcasted_iota(jnp.int32, (w_, w_), 0) // d_)
           == (lax.broadcasted_iota(jnp.int32, (w_, w_), 1) // d_)).astype(BF16)
    first_row = lax.broadcasted_iota(jnp.int32, (chunk, 1), 0) == 0

    def bd(x):
        return _stack_heads_masked(x, h_, d_).astype(BF16)

    def head_sum(x):
        hi, lo = _split(x)
        return (jnp.dot(hi, seg, preferred_element_type=F32)
                + jnp.dot(lo, seg, preferred_element_type=F32))

    def cumsum_rows(x):
        hi = x.astype(BF16)
        r1 = x - hi.astype(F32)
        mid = r1.astype(BF16)
        lo = (r1 - mid.astype(F32)).astype(BF16)
        return (jnp.dot(tri, hi, preferred_element_type=F32)
                + (jnp.dot(tri, mid, preferred_element_type=F32)
                   + jnp.dot(tri, lo, preferred_element_type=F32)))

    def tn(a, b):
        if a.shape[0] < LANES:
            pad = LANES - a.shape[0]
            a = jnp.concatenate([a, jnp.zeros((pad, a.shape[1]), F32)], axis=0)
            b = jnp.concatenate([b, jnp.zeros((pad, b.shape[1]), BF16)], axis=0)
        return _bdot(a.T, b)

    seqs = list(range(n_seq))
    p_rkv = [rkv_ref[i] for i in seqs]
    p_wdad = [wdad_ref[i] for i in seqs]
    m_rkv = [p + mu_rkv_ref[...] * (jnp.where(first_row, prev_rkv_ref[i], pltpu.roll(p, 1, 0)) - p)
             for i, p in zip(seqs, p_rkv)]
    m_wdad = [p + mu_wdad_ref[...] * (jnp.where(first_row, prev_wdad_ref[i], pltpu.roll(p, 1, 0)) - p)
              for i, p in zip(seqs, p_wdad)]
    r = [x[:, :w_] for x in m_rkv]
    k = [x[:, w_:2 * w_] for x in m_rkv]
    v = [x[:, 2 * w_:] for x in m_rkv]

    def log_decay(x):
        z = -(w0_ref[...] + _dot(jnp.tanh(x), w2_ref[...]))
        softplus = jnp.maximum(z, 0.0) + jnp.log(1.0 + jnp.exp(-jnp.abs(z)))
        return -jnp.exp(-softplus - 0.5)

    lw = _each(log_decay, m_wdad)
    a = _each(lambda x: _sigmoid(a0_ref[...] + _dot(x, a2_ref[...])), m_wdad)
    kk = _each(lambda x: x * kk_ref[...], k)
    kk = _each(lambda x: x / jnp.maximum(jnp.sqrt(head_sum(x * x)), 1e-12), kk)
    k_h = _each(lambda x, y: x * (1.0 + (y - 1.0) * ka_ref[...]), k, a)
    b = _each(lambda x, y: x * y, kk, a)
    bonus = _each(lambda x, y, z: head_sum(x * y * rk_ref[...]) * z, r, k_h, v)

    g = _each(cumsum_rows, lw)
    g_end = [x[chunk - 1:chunk, :] for x in g]
    k_til = _each(lambda x, y, z: bd(x * jnp.exp(y - z)), kk, g, lw)
    r_til = _each(lambda x, y: bd(x * jnp.exp(y)), r, g)
    inv_g = _each(lambda x: jnp.exp(-x), g)
    k_hat = _each(lambda x, y: bd(x * y), k_h, inv_g)
    b_hat = _each(lambda x, y: bd(x * y), b, inv_g)
    to_end = _each(lambda x, y: jnp.exp(x - y), g_end, g)
    k_bar = _each(lambda x, y: bd(x * y), k_h, to_end)
    b_bar = _each(lambda x, y: bd(x * y), b, to_end)
    v_f32 = _each(lambda x: _stack_heads_masked(x, h_, d_), v)
    v_bd = _each(lambda x: x.astype(BF16), v_f32)
    s_f32 = [s_ref[i] for i in seqs]
    s_bd = _each(lambda x: x.astype(BF16), s_f32)

    a_b = _each(lambda x, y: jnp.where(strict, _bdot(x, y, _NT), 0.0), k_til, b_hat)
    a_k = _each(lambda x, y: jnp.where(strict, _bdot(x, y, _NT), 0.0), k_til, k_hat)
    a_rk = _each(lambda x, y: jnp.where(incl, _bdot(x, y, _NT), 0.0), r_til, k_hat)
    a_rb = _each(lambda x, y: jnp.where(incl, _bdot(x, y, _NT), 0.0), r_til, b_hat)
    t_inv = _unit_lower_inverse(a_b, eye, chunk, row_t, col_t)
    rhs = _each(lambda x, y, z, w: _bdot(x, y, _NT) + _bdot(z, w), k_til, s_bd, a_k, v_bd)
    u_f32 = _each(_bdot, t_inv, rhs)
    u = _each(lambda x: x.astype(BF16), u_f32)
    o_bd = _each(lambda x, y, z, w, p, q: _bdot(x, y, _NT) + _bdot(z, w) - _bdot(p, q),
                 r_til, s_bd, a_rk, v_bd, a_rb, u)
    s_new = _each(lambda x, y, vf, kb, uf, bb: x * jnp.exp(y) + tn(vf, kb) - tn(uf, bb),
                  s_f32, g_end, v_f32, k_bar, u_f32, b_bar)

    def group_norm(o):
        out = o[0:chunk]
        for h in range(1, h_):
            out = out + o[h * chunk:(h + 1) * chunk]
        cen = out - head_sum(out) * (1.0 / d_)
        var = head_sum(cen * cen) * (1.0 / d_)
        return cen * lax.rsqrt(var + GN_EPS) * gng_ref[...] + gnb_ref[...]

    out = _each(group_norm, o_bd)
    for i in seqs:
        s_ref[i] = s_new[i]
        o_ref[i] = out[i] + bonus[i]
        prev_rkv_ref[i] = p_rkv[i][chunk - 1:chunk, :]
        prev_wdad_ref[i] = p_wdad[i][chunk - 1:chunk, :]


def _rwkv(rkv, wdad, prev_rkv, prev_wdad, s_bd, params):
    b, t, _ = rkv.shape
    chunk = min(RWKV_CHUNK, t)
    n_seq = next(n for n in (RWKV_SEQS_PER_STEP, 4, 2, 1) if b % n == 0)
    seq = lambda w: pl.BlockSpec((n_seq, chunk, w), lambda i, c: (i, c, 0))
    one = lambda w: pl.BlockSpec((n_seq, 1, w), lambda i, c: (i, 0, 0))
    full = lambda a: pl.BlockSpec(a.shape, lambda i, c: (0,) * a.ndim)
    st = pl.BlockSpec((n_seq, C_WIDTH, C_WIDTH), lambda i, c: (i, 0, 0))
    return pl.pallas_call(
        functools.partial(_rwkv_kernel, chunk=chunk, n_seq=n_seq),
        grid=(b // n_seq, t // chunk),
        in_specs=[seq(3 * C_WIDTH), seq(LANES), one(3 * C_WIDTH), one(LANES), st]
                 + [full(p) for p in params],
        out_specs=[seq(C_WIDTH), st],
        out_shape=[jax.ShapeDtypeStruct((b, t, C_WIDTH), F32),
                   jax.ShapeDtypeStruct((b, C_WIDTH, C_WIDTH), F32)],
        scratch_shapes=[pltpu.VMEM((n_seq, 1, 3 * C_WIDTH), F32), pltpu.VMEM((n_seq, 1, LANES), F32)],
        compiler_params=_cp(("arbitrary", "arbitrary")),
        name="rwkv7",
    )(rkv, wdad, prev_rkv, prev_wdad, s_bd, *params)


def _state_to_bd(state):
    b = state.shape[0]
    z = jnp.zeros((b, C_HEAD_DIM, C_HEAD_DIM), state.dtype)
    rows = [jnp.concatenate([state[:, h] if g == h else z for g in range(C_HEADS)], axis=2)
            for h in range(C_HEADS)]
    return jnp.concatenate(rows, axis=1)


def _state_from_bd(s_bd):
    d = C_HEAD_DIM
    return jnp.stack([s_bd[:, h * d:(h + 1) * d, h * d:(h + 1) * d] for h in range(C_HEADS)], axis=1)


def _out_kernel(x_ref, gate_ref, oa_ref, ga_ref, ob_ref, gb_ref, oc_ref, gc_ref, w_ref, lng_ref,
                lnb_ref, y_ref, *, alpha):
    mixed = (jnp.dot((oa_ref[...] * _silu(ga_ref[...])).astype(BF16), w_ref[0:A_WIDTH, :],
                     preferred_element_type=F32)
             + jnp.dot((ob_ref[...] * _silu(gb_ref[...])).astype(BF16),
                       w_ref[A_WIDTH:A_WIDTH + B_WIDTH, :], preferred_element_type=F32)
             + jnp.dot((oc_ref[...] * _silu(gc_ref[...])).astype(BF16),
                       w_ref[A_WIDTH + B_WIDTH:, :], preferred_element_type=F32))
    z = alpha * x_ref[...] + gate_ref[...] * mixed
    mu = jnp.mean(z, axis=-1, keepdims=True)
    zc = z - mu
    var = jnp.mean(zc * zc, axis=-1, keepdims=True)
    y_ref[...] = zc * lax.rsqrt(var + LN_EPS) * lng_ref[...] + lnb_ref[...]


def _out_proj(x2d, gate, tiles_per_mod, oa, ga, ob, gb, oc, gc, w_out, ln_g, ln_b, alpha, tm):
    r = x2d.shape[0]
    rb = gate.shape[1]
    row = lambda wd: pl.BlockSpec((tm, wd), lambda i: (i, 0))
    full = lambda a: pl.BlockSpec(a.shape, lambda i: (0,) * a.ndim)
    mod = pl.BlockSpec((None, rb, D_MODEL), lambda i: (i // tiles_per_mod, 0, 0))
    return pl.pallas_call(
        functools.partial(_out_kernel, alpha=alpha),
        grid=(r // tm,),
        in_specs=[row(D_MODEL), mod, row(A_WIDTH), row(A_WIDTH), row(B_WIDTH), row(B_WIDTH),
                  row(C_WIDTH), row(C_WIDTH), full(w_out), full(ln_g), full(ln_b)],
        out_specs=row(D_MODEL),
        out_shape=jax.ShapeDtypeStruct((r, D_MODEL), F32),
        compiler_params=_cp(("arbitrary",)),
        name="out_proj",
    )(x2d, gate, oa, ga, ob, gb, oc, gc, w_out, ln_g, ln_b)


def _page_copies(pt_ref, seq, first_page, n_pages, pools, bufs, sems, layer, slot):
    copies = []
    for j in range(n_pages):
        page = pt_ref[seq, first_page + j]
        for pool, buf, sem in zip(pools, bufs, sems):
            copies.append(pltpu.make_async_copy(pool.at[layer, page], buf.at[slot, j], sem.at[slot]))
    return copies


def _moba_sample_kernel(pt_ref, q_ref, kn_ref, vn_ref, expand_ref, pool_kt, pool_vt, o_ref,
                        buf, sem, kst_ref, m_ref, l_ref, acc_ref, *, pg, ng, n_blk, layer):
    b = pl.program_id(0)
    t = q_ref.shape[0]
    rows = A_HEADS * t
    per = MOBA_BLOCK // PAGE_SIZE
    bpg = pg // per

    def copies(seq, i):
        pool = pool_kt if i < ng else pool_vt
        return _page_copies(pt_ref, seq, (i % ng) * pg, pg, (pool,), (buf,), (sem,), layer, i % 2)

    def start(seq, i):
        for cp in copies(seq, i):
            cp.start()

    def wait(seq, i):
        for cp in copies(seq, i):
            cp.wait()

    @pl.when(b == 0)
    def _():
        start(b, 0)

    qm = _stack_heads_masked(q_ref[...], A_HEADS, A_HEAD_DIM)
    qb = (qm * (A_HEAD_DIM ** -0.5)).astype(BF16)

    lane = lax.broadcasted_iota(jnp.int32, (A_WIDTH, LANES), 1)
    kmean = jnp.zeros((A_WIDTH, LANES), F32)
    for g in range(ng):
        start(b, g + 1)
        wait(b, g)
        for i in range(bpg):
            blk_sum = buf[g % 2, i * per]
            for j in range(1, per):
                blk_sum = blk_sum + buf[g % 2, i * per + j]
            col = jnp.sum(blk_sum, axis=1, keepdims=True) * (1.0 / MOBA_BLOCK)
            kmean = jnp.where(lane == g * bpg + i, col, kmean)
        kst_ref[g] = jnp.concatenate([buf[g % 2, j].astype(BF16) for j in range(pg)], axis=1)

    gate = _doth(qm, kmean)
    blk = lax.broadcasted_iota(jnp.int32, gate.shape, 1)
    bias = jnp.where(_top_select(gate, blk < n_blk, 1), 0.0, NEG)

    _softmax_reset(m_ref, l_ref, acc_ref)
    s = _dot_nt(qb, kn_ref[...])
    qt = lax.broadcasted_iota(jnp.int32, (rows, t), 0) % t
    kt = lax.broadcasted_iota(jnp.int32, (rows, t), 1)
    _softmax_update(jnp.where(kt <= qt, s, NEG), m_ref, l_ref, acc_ref, vn_ref[...].astype(BF16), rows)

    r_i = lax.broadcasted_iota(jnp.int32, (LANES, LANES), 0)
    c_i = lax.broadcasted_iota(jnp.int32, (LANES, LANES), 1)
    for g in range(ng):
        if g + 1 < ng:
            start(b, ng + g + 1)
        else:
            @pl.when(b + 1 < pl.num_programs(0))
            def _():
                start(b + 1, 0)
        wait(b, ng + g)
        vc = jnp.concatenate([buf[(ng + g) % 2, j].astype(BF16) for j in range(pg)], axis=1)
        pick = ((r_i == g * bpg + c_i) & (c_i < bpg)).astype(BF16)
        bias_g = _dot(bias, pick)[:, :bpg].astype(BF16)
        s = (jnp.dot(qb, kst_ref[g], preferred_element_type=F32)
             + jnp.dot(bias_g, expand_ref[...], preferred_element_type=F32))
        _softmax_update(s, m_ref, l_ref, acc_ref, vc, rows, nt=True)

    o_ref[...] = _merge_heads(acc_ref[...] / l_ref[...], A_HEADS, A_HEAD_DIM)


def _moba_sample(q, k_new, v_new, pool_kt, pool_vt, page_table, layer, pg):
    db, t, w = q.shape
    n_pages = page_table.shape[1]
    n_blk = n_pages * PAGE_SIZE // MOBA_BLOCK
    assert n_blk <= LANES
    ng = n_pages // pg
    bpg = pg * PAGE_SIZE // MOBA_BLOCK
    rows = A_HEADS * t
    tok = pl.BlockSpec((None, t, w), lambda b, pt: (b, 0, 0))
    hbm = pl.BlockSpec(memory_space=pl.ANY)
    key_blk = jnp.arange(pg * PAGE_SIZE, dtype=jnp.int32) // MOBA_BLOCK
    expand = (jnp.arange(bpg, dtype=jnp.int32)[:, None] == key_blk[None, :]).astype(BF16)
    return pl.pallas_call(
        functools.partial(_moba_sample_kernel, pg=pg, ng=ng, n_blk=n_blk, layer=layer),
        grid_spec=pltpu.PrefetchScalarGridSpec(
            num_scalar_prefetch=1, grid=(db,),
            in_specs=[tok, tok, tok, pl.BlockSpec(expand.shape, lambda b, pt: (0, 0)), hbm, hbm],
            out_specs=tok,
            scratch_shapes=[pltpu.VMEM((2, pg, w, PAGE_SIZE), F32), pltpu.SemaphoreType.DMA((2,)),
                            pltpu.VMEM((ng, w, pg * PAGE_SIZE), BF16),
                            pltpu.VMEM((rows, 1), F32), pltpu.VMEM((rows, 1), F32),
                            pltpu.VMEM((rows, w), F32)]),
        out_shape=jax.ShapeDtypeStruct((db, t, w), F32),
        compiler_params=_cp(("arbitrary",)),
        name="moba_sample",
    )(page_table, q, k_new, v_new, expand, pool_kt, pool_vt)


def _mla_sample_kernel(pt_ref, qabs_ref, qrope_ref, cn_ref, rn_ref, wuv_ref, pool_c, pool_rt, o_ref,
                       cbuf, rbuf, csem, rsem, m_ref, l_ref, acc_ref, *, pg, layer):
    b, g = pl.program_id(0), pl.program_id(1)
    nb, ng = pl.num_programs(0), pl.num_programs(1)
    step = b * ng + g
    slot = step % 2
    t = qabs_ref.shape[0]
    rows = B_HEADS * t

    def group_copies(step_idx, slot_idx):
        return _page_copies(pt_ref, step_idx // ng, (step_idx % ng) * pg, pg, (pool_c, pool_rt),
                            (cbuf, rbuf), (csem, rsem), layer, slot_idx)

    @pl.when(step == 0)
    def _():
        for cp in group_copies(step, slot):
            cp.start()

    @pl.when(step + 1 < nb * ng)
    def _():
        for cp in group_copies(step + 1, 1 - slot):
            cp.start()

    qs, qr = _stack_mla_q(qabs_ref, qrope_ref)

    @pl.when(g == 0)
    def _():
        _softmax_reset(m_ref, l_ref, acc_ref)
        cn = cn_ref[...].astype(BF16)
        s = _dot_nt(qs, cn) + _dot_nt(qr, rn_ref[...])
        qt = lax.broadcasted_iota(jnp.int32, (rows, t), 0) % t
        kt = lax.broadcasted_iota(jnp.int32, (rows, t), 1)
        _softmax_update(jnp.where(kt <= qt, s, NEG), m_ref, l_ref, acc_ref, cn, rows)

    for cp in group_copies(step, slot):
        cp.wait()
    cc = cbuf[slot].reshape(pg * PAGE_SIZE, B_KV_RANK).astype(BF16)
    rc = jnp.concatenate([rbuf[slot, j].astype(BF16) for j in range(pg)], axis=1)
    s = _dot_nt(qs, cc) + jnp.dot(qr, rc, preferred_element_type=F32)
    _softmax_update(s, m_ref, l_ref, acc_ref, cc, rows)

    @pl.when(g == ng - 1)
    def _():
        _mla_finish(acc_ref[...], l_ref[...], wuv_ref, o_ref, t)


def _mla_sample(qabs, qrope, c_new, r_new, pool_c, pool_rt, page_table, wuv, layer, pg):
    db, t, _ = qabs.shape
    n_pages = page_table.shape[1]
    rows = B_HEADS * t
    tok = lambda w: pl.BlockSpec((None, t, w), lambda b, g, pt: (b, 0, 0))
    hbm = pl.BlockSpec(memory_space=pl.ANY)
    return pl.pallas_call(
        functools.partial(_mla_sample_kernel, pg=pg, layer=layer),
        grid_spec=pltpu.PrefetchScalarGridSpec(
            num_scalar_prefetch=1, grid=(db, n_pages // pg),
            in_specs=[tok(qabs.shape[2]), tok(qrope.shape[2]), tok(B_KV_RANK), tok(B_ROPE_DIM),
                      pl.BlockSpec(wuv.shape, lambda b, g, pt: (0, 0, 0)), hbm, hbm],
            out_specs=tok(B_WIDTH),
            scratch_shapes=[pltpu.VMEM((2, pg, PAGE_SIZE, B_KV_RANK), F32),
                            pltpu.VMEM((2, pg, B_ROPE_DIM, PAGE_SIZE), F32),
                            pltpu.SemaphoreType.DMA((2,)), pltpu.SemaphoreType.DMA((2,)),
                            pltpu.VMEM((rows, 1), F32), pltpu.VMEM((rows, 1), F32),
                            pltpu.VMEM((rows, B_KV_RANK), F32)]),
        out_shape=jax.ShapeDtypeStruct((db, t, B_WIDTH), F32),
        compiler_params=_cp(("arbitrary", "arbitrary")),
        name="mla_sample",
    )(page_table, qabs, qrope, c_new, r_new, wuv, pool_c, pool_rt)


def _prep_w_in(w):
    z = jnp.zeros((w.shape[0], 64), w.dtype)
    return jnp.concatenate(
        [w[:, 0:1536], w[:, 1600:2112], w[:, 2112:2880], w[:, 3008:3264], w[:, 1536:1600], z,
         w[:, 2880:3008]], axis=1).astype(BF16)


def _prep_w_uq(w):
    w = w.reshape(B_Q_RANK, B_HEADS, B_NOPE_DIM + B_ROPE_DIM)
    nope = w[:, :, :B_NOPE_DIM].reshape(B_Q_RANK, B_HEADS * B_NOPE_DIM)
    rope = jnp.pad(w[:, :, B_NOPE_DIM:], ((0, 0), (0, 0), (0, LANES - B_ROPE_DIM)))
    return jnp.concatenate([nope, rope.reshape(B_Q_RANK, B_HEADS * LANES)], axis=1).astype(BF16)


def _pad_rows_64(w, top):
    z = jnp.zeros((64, w.shape[1]), w.dtype)
    return jnp.concatenate([w, z] if top else [z, w], axis=0)


def kernel(x_prompt, x_sample, cache_moba_k, cache_moba_v, cache_mla_latent, cache_mla_rope, state_rwkv_wkv, state_rwkv_shift, page_table, c_prompt, c_sample, w_ada, b_ada, w_in, mla_q_norm, mla_kv_norm, mla_w_uq, mla_w_uk, mla_w_uv, rwkv_mu, rwkv_w0, rwkv_w2, rwkv_a0, rwkv_a2, rwkv_k_k, rwkv_k_a, rwkv_r_k, rwkv_gn_g, rwkv_gn_b, w_out, ln_g, ln_b):
    depth = w_in.shape[0]
    bp, sp, d = x_prompt.shape
    db, ts, _ = x_sample.shape
    n_pages = page_table.shape[1]
    n_pool = cache_moba_k.shape[1]
    past = n_pages * PAGE_SIZE
    alpha = (2 * depth) ** 0.25
    tm_p = min(256, sp)
    rs = db * ts
    tm_s = rs
    pg = min(32, n_pages)

    mods = _ada_all(jnp.concatenate([c_prompt, c_sample], axis=0), w_ada, b_ada)

    pos_p = jnp.arange(sp, dtype=jnp.int32)
    pos_s = jnp.tile(past + jnp.arange(ts, dtype=jnp.int32), db)
    tabs = [(_rope_tables(p, A_ROPE_DIM // 2, ROPE_THETA, A_HEAD_DIM, A_HEADS),
             _rope_tables(p, B_ROPE_DIM // 2, MLA_ROPE_THETA, LANES, 1)) for p in (pos_p, pos_s)]

    pool_kt = jnp.transpose(cache_moba_k, (0, 1, 3, 4, 2)).reshape(depth, n_pool, A_WIDTH, PAGE_SIZE)
    pool_vt = jnp.transpose(cache_moba_v, (0, 1, 3, 4, 2)).reshape(depth, n_pool, A_WIDTH, PAGE_SIZE)
    pool_rt = jnp.transpose(cache_mla_rope, (0, 1, 3, 2))

    xp = x_prompt.reshape(bp * sp, d)
    xs = x_sample.reshape(rs, d)
    zero_state = jnp.zeros((bp, C_WIDTH, C_WIDTH), F32)
    outs = [[] for _ in range(12)]

    for l in range(depth):
        w = _prep_w_in(w_in[l])
        wuq = _prep_w_uq(mla_w_uq[l])
        wuk = jnp.transpose(mla_w_uk[l], (1, 2, 0)).astype(BF16)
        wuv = jnp.transpose(mla_w_uv[l], (1, 0, 2)).astype(BF16)
        qn = mla_q_norm[l].reshape(1, -1)
        kvn = mla_kv_norm[l].reshape(1, -1)
        mu = rwkv_mu[l]
        row = lambda a: a.reshape(1, -1)
        rwkv_params = (row(mu[:3 * C_WIDTH]), row(mu[3 * C_WIDTH:]), row(rwkv_w0[l]),
                       _pad_rows_64(rwkv_w2[l], True).astype(BF16), row(rwkv_a0[l]),
                       _pad_rows_64(rwkv_a2[l], False).astype(BF16), row(rwkv_k_k[l]),
                       row(rwkv_k_a[l]), row(rwkv_r_k[l]), row(rwkv_gn_g[l]), row(rwkv_gn_b[l]))
        wo = w_out[l].astype(BF16)
        lng, lnb = row(ln_g[l]), row(ln_b[l])

        for stream in (0, 1):
            if stream == 0:
                x2d, nb_, t_, tm = xp, bp, sp, tm_p
                mod = mods[l, :bp].reshape(bp, 1, 3 * d)
                tiles_per_mod = sp // tm
                prev = jnp.zeros((bp, 3 * C_WIDTH + LANES), F32)
                s0 = zero_state
            else:
                x2d, nb_, t_, tm = xs, db, ts, tm_s
                mod = jnp.repeat(mods[l, bp:], ts, axis=0).reshape(1, rs, 3 * d)
                tiles_per_mod = 1
                prev = state_rwkv_shift[l]
                s0 = _state_to_bd(state_rwkv_wkv[l])
            shift, scale, gate = mod[..., :d], mod[..., d:2 * d], mod[..., 2 * d:]
            tab_a, tab_b = tabs[stream]
            (qa, ka, va, ga, qabs, qrope, ckv, kr, gb, rkv, wdad, gc) = _in_proj(
                x2d, scale, shift, tiles_per_mod, w, wuq, wuk, qn, kvn, tab_a, tab_b, tm)
            r3 = lambda a: a.reshape(nb_, t_, a.shape[-1])
            if stream == 0:
                oa, ob = _prompt_attn(r3(qa), r3(ka), r3(va), r3(qabs), r3(qrope), r3(ckv), r3(kr), wuv)
            else:
                oa = _moba_sample(r3(qa), r3(ka), r3(va), pool_kt, pool_vt, page_table, l, pg)
                ob = _mla_sample(r3(qabs), r3(qrope), r3(ckv), r3(kr), cache_mla_latent,
                                 pool_rt, page_table, wuv, l, pg)
            oc, s_new = _rwkv(r3(rkv), r3(wdad), prev[:, None, :3 * C_WIDTH],
                              prev[:, None, 3 * C_WIDTH:], s0, rwkv_params)
            y = _out_proj(x2d, gate, tiles_per_mod, oa.reshape(-1, A_WIDTH), ga,
                          ob.reshape(-1, B_WIDTH), gb, oc.reshape(-1, C_WIDTH), gc, wo, lng, lnb,
                          alpha, tm)
            last = jnp.concatenate([r3(rkv)[:, -1], r3(wdad)[:, -1]], axis=-1)
            leaves = (ka.reshape(nb_, t_, A_HEADS, A_HEAD_DIM), va.reshape(nb_, t_, A_HEADS, A_HEAD_DIM),
                      r3(ckv), r3(kr), _state_from_bd(s_new), last)
            for i, leaf in enumerate(leaves):
                outs[stream * 6 + i].append(leaf)
            if stream == 0:
                xp = y
            else:
                xs = y

    return (xp.reshape(bp, sp, d), xs.reshape(db, ts, d)) + tuple(jnp.stack(o) for o in outs)
```

```python
import functools
import math

import jax
import jax.numpy as jnp
from jax import lax
from jax.experimental import pallas as pl
from jax.experimental.pallas import tpu as pltpu

F32 = jnp.float32
BF16 = jnp.bfloat16
HI = lax.Precision.HIGHEST

D_MODEL = 1024
PAGE_SIZE = 128
LANES = 128
DMA_PRIORITY_THREADS = 2
A_HEADS = 4
A_HEAD_DIM = 64
A_WIDTH = 256
A_ROPE_DIM = 16
ROPE_THETA = 500000.0
MOBA_BLOCK = 256
MOBA_TOPK = 3
Q_BLOCK = 256
B_HEADS = 4
B_NOPE_DIM = 128
B_ROPE_DIM = 64
B_V_DIM = 128
B_WIDTH = 512
B_Q_RANK = 256
B_KV_RANK = 256
MLA_ROPE_THETA = 10000.0
MLA_SCALE = (B_NOPE_DIM + B_ROPE_DIM) ** -0.5
C_HEADS = 4
C_HEAD_DIM = 64
C_WIDTH = 256
C_DECAY_RANK = 64
C_A_RANK = 64
GN_EPS = 64e-5
RWKV_CHUNK = 64
RWKV_SUB = 16
RWKV_SEQS_PER_STEP = 8
LN_EPS = 1e-5
RMS_EPS = 1e-6
NEG = -1e30

OFF_QA, OFF_KA, OFF_VA, OFF_GA = 0, 256, 512, 768
OFF_CQ, OFF_CKV, OFF_GB, OFF_RKV, OFF_GC, OFF_KR, OFF_WDAD = 1024, 1280, 1536, 2048, 2816, 3072, 3200
IN_COLS_PAD = 3328

VMEM_LIMIT = 56 * 1024 * 1024


def _cp(sem):
    return pltpu.CompilerParams(dimension_semantics=sem, vmem_limit_bytes=VMEM_LIMIT)


def _dot(a, b):
    return jnp.dot(a.astype(BF16), b.astype(BF16), preferred_element_type=F32)


def _dot_nt(a, b):
    return lax.dot_general(a.astype(BF16), b.astype(BF16), (((1,), (1,)), ((), ())),
                           preferred_element_type=F32)


def _doth(a, b):
    return jnp.dot(a, b, precision=HI, preferred_element_type=F32)


def _doth_nt(a, b):
    return lax.dot_general(a, b, (((1,), (1,)), ((), ())), precision=HI, preferred_element_type=F32)


_NN = (((1,), (0,)), ((), ()))
_NT = (((1,), (1,)), ((), ()))


def _split(x):
    hi = x.astype(BF16)
    return hi, (x - hi.astype(F32)).astype(BF16)


def _bdot(a, b, dims=_NN):
    return lax.dot_general(a.astype(BF16), b.astype(BF16), dims, preferred_element_type=F32)


def _sigmoid(x):
    return 1.0 / (1.0 + jnp.exp(-x))


def _silu(x):
    return x * _sigmoid(x)


def _head_mask(rows, width, head, head_dim):
    lane = lax.broadcasted_iota(jnp.int32, (rows, width), 1)
    return (lane >= head * head_dim) & (lane < (head + 1) * head_dim)


def _ada_kernel(c_ref, w_ref, b_ref, o_ref):
    o_ref[...] = _dot(_silu(c_ref[...]), w_ref[...]) + b_ref[...]


def _ada_all(c_all, w_ada, b_ada):
    depth = w_ada.shape[0]
    r = c_all.shape[0]
    nj = w_ada.shape[2] // D_MODEL
    return pl.pallas_call(
        _ada_kernel,
        grid=(depth, nj),
        in_specs=[pl.BlockSpec((r, D_MODEL), lambda l, j: (0, 0)),
                  pl.BlockSpec((None, D_MODEL, D_MODEL), lambda l, j: (l, 0, j)),
                  pl.BlockSpec((None, 1, D_MODEL), lambda l, j: (l, 0, j))],
        out_specs=pl.BlockSpec((None, r, D_MODEL), lambda l, j: (l, 0, j)),
        out_shape=jax.ShapeDtypeStruct((depth, r, w_ada.shape[2]), F32),
        compiler_params=_cp(("arbitrary", "arbitrary")),
        name="adaln_mod",
    )(c_all, w_ada, b_ada.reshape(depth, 1, -1))


def _rope_tables(pos, rot_half, theta, head_width, n_rep):
    inv = jnp.power(theta, -jnp.arange(rot_half, dtype=F32) / rot_half)
    ang = pos.astype(F32)[:, None] * inv[None, :]
    cos, sin = jnp.cos(ang), jnp.sin(ang)
    p = pos.shape[0]
    rest = head_width - 2 * rot_half
    c = jnp.concatenate([cos, cos, jnp.ones((p, rest), F32)], axis=1)
    s1 = jnp.concatenate([-sin, jnp.zeros((p, head_width - rot_half), F32)], axis=1)
    s2 = jnp.concatenate([jnp.zeros((p, rot_half), F32), sin, jnp.zeros((p, rest), F32)], axis=1)
    return jnp.stack([jnp.tile(t, (1, n_rep)) for t in (c, s1, s2)])


def _apply_rope(x, cos, s1, s2, half):
    w = x.shape[-1]
    return x * cos + pltpu.roll(x, w - half, 1) * s1 + pltpu.roll(x, half, 1) * s2


def _rms(x, g):
    return x * lax.rsqrt(jnp.mean(x * x, axis=-1, keepdims=True) + RMS_EPS) * g


def _in_kernel(x_ref, sc_ref, sh_ref, w_ref, wuq_ref, wuk_ref, qn_ref, kvn_ref, ta_ref, tb_ref,
               qa_ref, ka_ref, va_ref, ga_ref, qabs_ref, qrope_ref, ckv_ref, kr_ref, gb_ref,
               rkv_ref, wdad_ref, gc_ref):
    hb = (x_ref[...] * (1.0 + sc_ref[...]) + sh_ref[...]).astype(BF16)

    def proj(off, width):
        return jnp.dot(hb, w_ref[:, off:off + width], preferred_element_type=F32)

    ca, sa1, sa2 = ta_ref[0], ta_ref[1], ta_ref[2]
    cb, sb1, sb2 = tb_ref[0], tb_ref[1], tb_ref[2]
    qa_ref[...] = _apply_rope(proj(OFF_QA, A_WIDTH), ca, sa1, sa2, A_ROPE_DIM // 2)
    ka_ref[...] = _apply_rope(proj(OFF_KA, A_WIDTH), ca, sa1, sa2, A_ROPE_DIM // 2)
    va_ref[...] = proj(OFF_VA, A_WIDTH)
    ga_ref[...] = proj(OFF_GA, A_WIDTH)
    cqn = _rms(proj(OFF_CQ, B_Q_RANK), qn_ref[...])
    qf = _dot(cqn, wuq_ref[...])
    for h in range(B_HEADS):
        qabs_ref[:, h * B_KV_RANK:(h + 1) * B_KV_RANK] = _dot(
            qf[:, h * B_NOPE_DIM:(h + 1) * B_NOPE_DIM], wuk_ref[h])
        base = B_HEADS * B_NOPE_DIM + h * LANES
        qrope_ref[:, h * LANES:(h + 1) * LANES] = _apply_rope(
            qf[:, base:base + LANES], cb, sb1, sb2, B_ROPE_DIM // 2)
    ckv_ref[...] = _rms(proj(OFF_CKV, B_KV_RANK), kvn_ref[...])
    kr_ref[...] = _apply_rope(proj(OFF_KR, LANES), cb, sb1, sb2, B_ROPE_DIM // 2)[:, :B_ROPE_DIM]
    gb_ref[...] = proj(OFF_GB, B_WIDTH)
    rkv_ref[...] = proj(OFF_RKV, 3 * C_WIDTH)
    wdad_ref[...] = proj(OFF_WDAD, LANES)
    gc_ref[...] = proj(OFF_GC, C_WIDTH)


def _in_proj(x2d, scale, shift, tiles_per_mod, w, wuq, wuk, qn, kvn, tab_a, tab_b, tm):
    r = x2d.shape[0]
    n_tiles = r // tm
    tab_tiles = tab_a.shape[1] // tm
    rb = scale.shape[1]
    widths = [A_WIDTH, A_WIDTH, A_WIDTH, A_WIDTH, B_HEADS * B_KV_RANK, B_HEADS * LANES, B_KV_RANK,
              B_ROPE_DIM, B_WIDTH, 3 * C_WIDTH, LANES, C_WIDTH]
    row = lambda wd: pl.BlockSpec((tm, wd), lambda i: (i, 0))
    full = lambda a: pl.BlockSpec(a.shape, lambda i: (0,) * a.ndim)
    mod = pl.BlockSpec((None, rb, D_MODEL), lambda i: (i // tiles_per_mod, 0, 0))
    tab = lambda a: pl.BlockSpec((3, tm, a.shape[2]), lambda i: (0, i % tab_tiles, 0))
    return pl.pallas_call(
        _in_kernel,
        grid=(n_tiles,),
        in_specs=[row(D_MODEL), mod, mod, full(w), full(wuq), full(wuk), full(qn), full(kvn),
                  tab(tab_a), tab(tab_b)],
        out_specs=[row(wd) for wd in widths],
        out_shape=[jax.ShapeDtypeStruct((r, wd), F32) for wd in widths],
        compiler_params=_cp(("arbitrary",)),
        name="in_proj",
    )(x2d, scale, shift, w, wuq, wuk, qn, kvn, tab_a, tab_b)


def _top_select(g, valid, axis):
    n = g.shape[axis]
    idx = lax.broadcasted_iota(jnp.int32, g.shape, axis).astype(F32)
    g = jnp.where(valid, g, -jnp.inf)
    sel = jnp.zeros(g.shape, jnp.bool_)
    for _ in range(min(MOBA_TOPK, n)):
        mx = jnp.max(g, axis=axis, keepdims=True)
        first = jnp.min(jnp.where(g == mx, idx, float(n)), axis=axis, keepdims=True)
        hit = idx == first
        sel = sel | hit
        g = jnp.where(hit, -jnp.inf, g)
    return sel & valid


def _stack_heads_masked(q, n_heads, head_dim):
    r, w = q.shape
    return jnp.concatenate(
        [jnp.where(_head_mask(r, w, h, head_dim), q, 0.0) for h in range(n_heads)], axis=0)


def _merge_heads(acc, n_heads, head_dim):
    r = acc.shape[0] // n_heads
    w = acc.shape[1]
    out = jnp.zeros((r, w), F32)
    for h in range(n_heads):
        out = out + jnp.where(_head_mask(r, w, h, head_dim), acc[h * r:(h + 1) * r], 0.0)
    return out


def _softmax_reset(m_ref, l_ref, acc_ref):
    m_ref[...] = jnp.full(m_ref.shape, NEG, F32)
    l_ref[...] = jnp.zeros(l_ref.shape, F32)
    acc_ref[...] = jnp.zeros(acc_ref.shape, F32)


def _softmax_update(s, m_ref, l_ref, acc_ref, v, group, nt=False):
    ps = []
    for g0 in range(0, s.shape[0], group):
        sl = slice(g0, g0 + group)
        sg = s[sl]
        m_old = m_ref[sl]
        m_new = jnp.maximum(m_old, jnp.max(sg, axis=-1, keepdims=True))
        alpha = jnp.exp(m_old - m_new)
        p = jnp.exp(sg - m_new)
        l_ref[sl] = alpha * l_ref[sl] + jnp.sum(p, axis=-1, keepdims=True)
        m_ref[sl] = m_new
        acc_ref[sl] = alpha * acc_ref[sl]
        ps.append(p.astype(BF16))
    p_all = ps[0] if len(ps) == 1 else jnp.concatenate(ps, axis=0)
    acc_ref[...] += lax.dot_general(p_all, v, _NT if nt else _NN, preferred_element_type=F32)


def _stack_mla_q(qabs_ref, qrope_ref):
    qs = jnp.concatenate([qabs_ref[:, h * B_KV_RANK:(h + 1) * B_KV_RANK] for h in range(B_HEADS)], axis=0)
    qr = jnp.concatenate([qrope_ref[:, h * LANES:h * LANES + B_ROPE_DIM] for h in range(B_HEADS)], axis=0)
    return (qs * MLA_SCALE).astype(BF16), (qr * MLA_SCALE).astype(BF16)


def _mla_finish(acc, l, wuv_ref, o_ref, r):
    o = acc / l
    for h in range(B_HEADS):
        o_ref[:, h * B_V_DIM:(h + 1) * B_V_DIM] = _dot(o[h * r:(h + 1) * r], wuv_ref[h])


def _prompt_attn_kernel(q_ref, k_ref, v_ref, qabs_ref, qrope_ref, ckv_ref, kr_ref, wuv_ref,
                        oa_ref, ob_ref, kmean_ref, vt_ref, bias_ref, acca_ref, ckvt_ref, accb_ref, *, nb):
    c = pl.program_id(1)
    blk_n = MOBA_BLOCK
    nbp = kmean_ref.shape[0]
    rows = A_HEADS * Q_BLOCK

    @pl.when(c == 0)
    def _():
        means = [jnp.mean(k_ref[n * blk_n:(n + 1) * blk_n, :], axis=0, keepdims=True) for n in range(nb)]
        pad = [jnp.zeros((nbp - nb, A_WIDTH), F32)] if nbp > nb else []
        kmean_ref[...] = jnp.concatenate(means + pad, axis=0)
        for n in range(nb):
            vt_ref[n] = v_ref[n * blk_n:(n + 1) * blk_n, :].T.astype(BF16)
            ckvt_ref[n] = ckv_ref[n * blk_n:(n + 1) * blk_n, :].T.astype(BF16)

    qm = _stack_heads_masked(q_ref[...], A_HEADS, A_HEAD_DIM)
    gate_t = _doth_nt(kmean_ref[...], qm)
    blk = lax.broadcasted_iota(jnp.int32, gate_t.shape, 0)
    bias_ref[...] = jnp.where(_top_select(gate_t, blk < c, 0), 0.0, NEG)
    qb = (qm * (A_HEAD_DIM ** -0.5)).astype(BF16)
    qs, qr = _stack_mla_q(qabs_ref, qrope_ref)
    acca_ref[...] = jnp.zeros(acca_ref.shape, F32)
    accb_ref[...] = jnp.zeros(accb_ref.shape, F32)
    qpos = c * Q_BLOCK + lax.broadcasted_iota(jnp.int32, (1, rows), 1) % Q_BLOCK

    def block(n, carry, own_block):
        ma, la, mb, lb = carry
        start = pl.multiple_of(n * blk_n, blk_n)
        sa = _dot_nt(k_ref[pl.ds(start, blk_n), :], qb)
        sb = (_dot_nt(ckv_ref[pl.ds(start, blk_n), :], qs)
              + _dot_nt(kr_ref[pl.ds(start, blk_n), :], qr))
        if own_block:
            allowed = (n * blk_n + lax.broadcasted_iota(jnp.int32, (blk_n, 1), 0)) <= qpos
            sa = jnp.where(allowed, sa, NEG)
            sb = jnp.where(allowed, sb, NEG)
        else:
            sa = sa + bias_ref[pl.ds(n, 1), :]
        ma_new = jnp.maximum(ma, jnp.max(sa, axis=0, keepdims=True))
        mb_new = jnp.maximum(mb, jnp.max(sb, axis=0, keepdims=True))
        alpha_a = jnp.exp(ma - ma_new)
        alpha_b = jnp.exp(mb - mb_new)
        pa = jnp.exp(sa - ma_new)
        pb = jnp.exp(sb - mb_new)
        la = alpha_a * la + jnp.sum(pa, axis=0, keepdims=True)
        lb = alpha_b * lb + jnp.sum(pb, axis=0, keepdims=True)
        pa = pa.astype(BF16)
        for h in range(A_HEADS):
            cols = slice(h * Q_BLOCK, (h + 1) * Q_BLOCK)
            acca_ref[h] = acca_ref[h] * alpha_a[:, cols] + jnp.dot(
                vt_ref[n, h * A_HEAD_DIM:(h + 1) * A_HEAD_DIM, :], pa[:, cols],
                preferred_element_type=F32)
        accb_ref[...] = accb_ref[...] * alpha_b + jnp.dot(ckvt_ref[n], pb.astype(BF16),
                                                          preferred_element_type=F32)
        return ma_new, la, mb_new, lb

    neg = jnp.full((1, rows), NEG, F32)
    zero = jnp.zeros((1, rows), F32)
    carry = block(c, (neg, zero, neg, zero), True)
    ma, la, mb, lb = lax.fori_loop(0, c, lambda n, cr: block(n, cr, False), carry)

    inv_la = 1.0 / la
    merged_t = jnp.concatenate(
        [acca_ref[h] * inv_la[:, h * Q_BLOCK:(h + 1) * Q_BLOCK] for h in range(A_HEADS)], axis=0)
    oa_ref[...] = merged_t.T
    o_t = accb_ref[...] * (1.0 / lb)
    for h in range(B_HEADS):
        o_h = o_t[:, h * Q_BLOCK:(h + 1) * Q_BLOCK].T
        ob_ref[:, h * B_V_DIM:(h + 1) * B_V_DIM] = _dot(o_h, wuv_ref[h])


def _prompt_attn(q, k, v, qabs, qrope, ckv, kr, wuv):
    b, s, w = q.shape
    assert Q_BLOCK == MOBA_BLOCK and s % MOBA_BLOCK == 0
    nb = s // MOBA_BLOCK
    nbp = -(-nb // 8) * 8
    rows = A_HEADS * Q_BLOCK
    blk = lambda wd: pl.BlockSpec((None, Q_BLOCK, wd), lambda i, c: (i, c, 0))
    seq = lambda wd: pl.BlockSpec((None, s, wd), lambda i, c: (i, 0, 0))
    return pl.pallas_call(
        functools.partial(_prompt_attn_kernel, nb=nb),
        grid=(b, nb),
        in_specs=[blk(w), seq(w), seq(w), blk(qabs.shape[2]), blk(qrope.shape[2]), seq(B_KV_RANK),
                  seq(B_ROPE_DIM), pl.BlockSpec(wuv.shape, lambda i, c: (0, 0, 0))],
        out_specs=[blk(w), blk(B_WIDTH)],
        out_shape=[jax.ShapeDtypeStruct((b, s, w), F32), jax.ShapeDtypeStruct((b, s, B_WIDTH), F32)],
        scratch_shapes=[pltpu.VMEM((nbp, w), F32), pltpu.VMEM((nb, w, MOBA_BLOCK), BF16),
                        pltpu.VMEM((nbp, rows), F32), pltpu.VMEM((A_HEADS, A_HEAD_DIM, Q_BLOCK), F32),
                        pltpu.VMEM((nb, B_KV_RANK, MOBA_BLOCK), BF16),
                        pltpu.VMEM((B_KV_RANK, rows), F32)],
        compiler_params=_cp(("arbitrary", "arbitrary")),
        name="prompt_attn",
    )(q, k, v, qabs, qrope, ckv, kr, wuv)


def _each(f, *lists):
    return [f(*xs) for xs in zip(*lists)]


def _neumann(a, eye, levels):
    t = _each(lambda x: eye - x, a)
    p = _each(lambda x: x.astype(BF16), a)
    for _ in range(levels - 1):
        p = _each(lambda x: _bdot(x, x).astype(BF16), p)
        t = _each(lambda y, x: y + _bdot(y, x), t, p)
    return t


def _unit_lower_inverse(a, eye, chunk, row_t, col_t):
    if chunk <= RWKV_SUB:
        return _neumann(a, eye, max(1, math.ceil(math.log2(chunk))))
    near = (row_t // RWKV_SUB) == (col_t // RWKV_SUB)
    a_d = _each(lambda x: jnp.where(near, x, 0.0), a)
    t_d = _each(lambda x: x.astype(BF16), _neumann(a_d, eye, int(math.log2(RWKV_SUB))))
    n = _each(lambda t, x, xd: _bdot(t, x - xd), t_d, a, a_d)
    t_n = _neumann(n, eye, max(1, math.ceil(math.log2(chunk // RWKV_SUB))))
    return _each(_bdot, t_n, t_d)


def _rwkv_kernel(rkv_ref, wdad_ref, prkv_ref, pwdad_ref, s0_ref, mu_rkv_ref, mu_wdad_ref,
                 w0_ref, w2_ref, a0_ref, a2_ref, kk_ref, ka_ref, rk_ref, gng_ref, gnb_ref,
                 o_ref, s_ref, prev_rkv_ref, prev_wdad_ref, *, chunk, n_seq):
    h_, d_, w_ = C_HEADS, C_HEAD_DIM, C_WIDTH
    n = h_ * chunk

    @pl.when(pl.program_id(1) == 0)
    def _():
        s_ref[...] = s0_ref[...]
        prev_rkv_ref[...] = prkv_ref[...]
        prev_wdad_ref[...] = pwdad_ref[...]

    ri = lax.broadcasted_iota(jnp.int32, (n, n), 0)
    ci = lax.broadcasted_iota(jnp.int32, (n, n), 1)
    same = (ri // chunk) == (ci // chunk)
    row_t, col_t = ri % chunk, ci % chunk
    strict = (same & (row_t > col_t))
    incl = (same & (row_t >= col_t))
    eye = (ri == ci).astype(F32)
    tri = (lax.broadcasted_iota(jnp.int32, (chunk, chunk), 0)
           >= lax.broadcasted_iota(jnp.int32, (chunk, chunk), 1)).astype(BF16)
    seg = ((lax.broadcasted_iota(jnp.int32, (w_, w_), 0) // d_)
           == (lax.broadcasted_iota(jnp.int32, (w_, w_), 1) // d_)).astype(BF16)
    first_row = lax.broadcasted_iota(jnp.int32, (chunk, 1), 0) == 0

    def bd(x):
        return _stack_heads_masked(x, h_, d_).astype(BF16)

    def head_sum(x):
        hi, lo = _split(x)
        return (jnp.dot(hi, seg, preferred_element_type=F32)
                + jnp.dot(lo, seg, preferred_element_type=F32))

    def cumsum_rows(x):
        hi = x.astype(BF16)
        r1 = x - hi.astype(F32)
        mid = r1.astype(BF16)
        lo = (r1 - mid.astype(F32)).astype(BF16)
        return (jnp.dot(tri, hi, preferred_element_type=F32)
                + (jnp.dot(tri, mid, preferred_element_type=F32)
                   + jnp.dot(tri, lo, preferred_element_type=F32)))

    def tn(a, b):
        if a.shape[0] < LANES:
            pad = LANES - a.shape[0]
            a = jnp.concatenate([a, jnp.zeros((pad, a.shape[1]), F32)], axis=0)
            b = jnp.concatenate([b, jnp.zeros((pad, b.shape[1]), BF16)], axis=0)
        return _bdot(a.T, b)

    seqs = list(range(n_seq))
    p_rkv = [rkv_ref[i] for i in seqs]
    p_wdad = [wdad_ref[i] for i in seqs]
    m_rkv = [p + mu_rkv_ref[...] * (jnp.where(first_row, prev_rkv_ref[i], pltpu.roll(p, 1, 0)) - p)
             for i, p in zip(seqs, p_rkv)]
    m_wdad = [p + mu_wdad_ref[...] * (jnp.where(first_row, prev_wdad_ref[i], pltpu.roll(p, 1, 0)) - p)
              for i, p in zip(seqs, p_wdad)]
    r = [x[:, :w_] for x in m_rkv]
    k = [x[:, w_:2 * w_] for x in m_rkv]
    v = [x[:, 2 * w_:] for x in m_rkv]

    def log_decay(x):
        z = -(w0_ref[...] + _dot(jnp.tanh(x), w2_ref[...]))
        softplus = jnp.maximum(z, 0.0) + jnp.log(1.0 + jnp.exp(-jnp.abs(z)))
        return -jnp.exp(-softplus - 0.5)

    lw = _each(log_decay, m_wdad)
    a = _each(lambda x: _sigmoid(a0_ref[...] + _dot(x, a2_ref[...])), m_wdad)
    kk = _each(lambda x: x * kk_ref[...], k)
    kk = _each(lambda x: x / jnp.maximum(jnp.sqrt(head_sum(x * x)), 1e-12), kk)
    k_h = _each(lambda x, y: x * (1.0 + (y - 1.0) * ka_ref[...]), k, a)
    b = _each(lambda x, y: x * y, kk, a)
    bonus = _each(lambda x, y, z: head_sum(x * y * rk_ref[...]) * z, r, k_h, v)

    g = _each(cumsum_rows, lw)
    g_end = [x[chunk - 1:chunk, :] for x in g]
    k_til = _each(lambda x, y, z: bd(x * jnp.exp(y - z)), kk, g, lw)
    r_til = _each(lambda x, y: bd(x * jnp.exp(y)), r, g)
    inv_g = _each(lambda x: jnp.exp(-x), g)
    k_hat = _each(lambda x, y: bd(x * y), k_h, inv_g)
    b_hat = _each(lambda x, y: bd(x * y), b, inv_g)
    to_end = _each(lambda x, y: jnp.exp(x - y), g_end, g)
    k_bar = _each(lambda x, y: bd(x * y), k_h, to_end)
    b_bar = _each(lambda x, y: bd(x * y), b, to_end)
    v_f32 = _each(lambda x: _stack_heads_masked(x, h_, d_), v)
    v_bd = _each(lambda x: x.astype(BF16), v_f32)
    s_f32 = [s_ref[i] for i in seqs]
    s_bd = _each(lambda x: x.astype(BF16), s_f32)

    a_b = _each(lambda x, y: jnp.where(strict, _bdot(x, y, _NT), 0.0), k_til, b_hat)
    a_k = _each(lambda x, y: jnp.where(strict, _bdot(x, y, _NT), 0.0), k_til, k_hat)
    a_rk = _each(lambda x, y: jnp.where(incl, _bdot(x, y, _NT), 0.0), r_til, k_hat)
    a_rb = _each(lambda x, y: jnp.where(incl, _bdot(x, y, _NT), 0.0), r_til, b_hat)
    t_inv = _unit_lower_inverse(a_b, eye, chunk, row_t, col_t)
    rhs = _each(lambda x, y, z, w: _bdot(x, y, _NT) + _bdot(z, w), k_til, s_bd, a_k, v_bd)
    u_f32 = _each(_bdot, t_inv, rhs)
    u = _each(lambda x: x.astype(BF16), u_f32)
    o_bd = _each(lambda x, y, z, w, p, q: _bdot(x, y, _NT) + _bdot(z, w) - _bdot(p, q),
                 r_til, s_bd, a_rk, v_bd, a_rb, u)
    s_new = _each(lambda x, y, vf, kb, uf, bb: x * jnp.exp(y) + tn(vf, kb) - tn(uf, bb),
                  s_f32, g_end, v_f32, k_bar, u_f32, b_bar)

    def group_norm(o):
        out = o[0:chunk]
        for h in range(1, h_):
            out = out + o[h * chunk:(h + 1) * chunk]
        cen = out - head_sum(out) * (1.0 / d_)
        var = head_sum(cen * cen) * (1.0 / d_)
        return cen * lax.rsqrt(var + GN_EPS) * gng_ref[...] + gnb_ref[...]

    out = _each(group_norm, o_bd)
    for i in seqs:
        s_ref[i] = s_new[i]
        o_ref[i] = out[i] + bonus[i]
        prev_rkv_ref[i] = p_rkv[i][chunk - 1:chunk, :]
        prev_wdad_ref[i] = p_wdad[i][chunk - 1:chunk, :]


def _rwkv(rkv, wdad, prev_rkv, prev_wdad, s_bd, params):
    b, t, _ = rkv.shape
    chunk = min(RWKV_CHUNK, t)
    n_seq = next(n for n in (RWKV_SEQS_PER_STEP, 4, 2, 1) if b % n == 0)
    seq = lambda w: pl.BlockSpec((n_seq, chunk, w), lambda i, c: (i, c, 0))
    one = lambda w: pl.BlockSpec((n_seq, 1, w), lambda i, c: (i, 0, 0))
    full = lambda a: pl.BlockSpec(a.shape, lambda i, c: (0,) * a.ndim)
    st = pl.BlockSpec((n_seq, C_WIDTH, C_WIDTH), lambda i, c: (i, 0, 0))
    return pl.pallas_call(
        functools.partial(_rwkv_kernel, chunk=chunk, n_seq=n_seq),
        grid=(b // n_seq, t // chunk),
        in_specs=[seq(3 * C_WIDTH), seq(LANES), one(3 * C_WIDTH), one(LANES), st]
                 + [full(p) for p in params],
        out_specs=[seq(C_WIDTH), st],
        out_shape=[jax.ShapeDtypeStruct((b, t, C_WIDTH), F32),
                   jax.ShapeDtypeStruct((b, C_WIDTH, C_WIDTH), F32)],
        scratch_shapes=[pltpu.VMEM((n_seq, 1, 3 * C_WIDTH), F32), pltpu.VMEM((n_seq, 1, LANES), F32)],
        compiler_params=_cp(("arbitrary", "arbitrary")),
        name="rwkv7",
    )(rkv, wdad, prev_rkv, prev_wdad, s_bd, *params)


def _state_to_bd(state):
    b = state.shape[0]
    z = jnp.zeros((b, C_HEAD_DIM, C_HEAD_DIM), state.dtype)
    rows = [jnp.concatenate([state[:, h] if g == h else z for g in range(C_HEADS)], axis=2)
            for h in range(C_HEADS)]
    return jnp.concatenate(rows, axis=1)


def _state_from_bd(s_bd):
    d = C_HEAD_DIM
    return jnp.stack([s_bd[:, h * d:(h + 1) * d, h * d:(h + 1) * d] for h in range(C_HEADS)], axis=1)


def _out_kernel(x_ref, gate_ref, oa_ref, ga_ref, ob_ref, gb_ref, oc_ref, gc_ref, w_ref, lng_ref,
                lnb_ref, y_ref, *, alpha):
    mixed = (jnp.dot((oa_ref[...] * _silu(ga_ref[...])).astype(BF16), w_ref[0:A_WIDTH, :],
                     preferred_element_type=F32)
             + jnp.dot((ob_ref[...] * _silu(gb_ref[...])).astype(BF16),
                       w_ref[A_WIDTH:A_WIDTH + B_WIDTH, :], preferred_element_type=F32)
             + jnp.dot((oc_ref[...] * _silu(gc_ref[...])).astype(BF16),
                       w_ref[A_WIDTH + B_WIDTH:, :], preferred_element_type=F32))
    z = alpha * x_ref[...] + gate_ref[...] * mixed
    mu = jnp.mean(z, axis=-1, keepdims=True)
    zc = z - mu
    var = jnp.mean(zc * zc, axis=-1, keepdims=True)
    y_ref[...] = zc * lax.rsqrt(var + LN_EPS) * lng_ref[...] + lnb_ref[...]


def _out_proj(x2d, gate, tiles_per_mod, oa, ga, ob, gb, oc, gc, w_out, ln_g, ln_b, alpha, tm):
    r = x2d.shape[0]
    rb = gate.shape[1]
    row = lambda wd: pl.BlockSpec((tm, wd), lambda i: (i, 0))
    full = lambda a: pl.BlockSpec(a.shape, lambda i: (0,) * a.ndim)
    mod = pl.BlockSpec((None, rb, D_MODEL), lambda i: (i // tiles_per_mod, 0, 0))
    return pl.pallas_call(
        functools.partial(_out_kernel, alpha=alpha),
        grid=(r // tm,),
        in_specs=[row(D_MODEL), mod, row(A_WIDTH), row(A_WIDTH), row(B_WIDTH), row(B_WIDTH),
                  row(C_WIDTH), row(C_WIDTH), full(w_out), full(ln_g), full(ln_b)],
        out_specs=row(D_MODEL),
        out_shape=jax.ShapeDtypeStruct((r, D_MODEL), F32),
        compiler_params=_cp(("arbitrary",)),
        name="out_proj",
    )(x2d, gate, oa, ga, ob, gb, oc, gc, w_out, ln_g, ln_b)


def _page_copies(pt_ref, seq, first_page, n_pages, pools, bufs, sems, layer, slot):
    copies = []
    for j in range(n_pages):
        page = pt_ref[seq, first_page + j]
        for pool, buf, sem in zip(pools, bufs, sems):
            copies.append((pltpu.make_async_copy(pool.at[layer, page], buf.at[slot, j], sem.at[slot]),
                           j % DMA_PRIORITY_THREADS))
    return copies


def _start_all(copies):
    for cp, priority in copies:
        cp.start(priority=priority)


def _wait_all(copies):
    for cp, _ in copies:
        cp.wait()


def _moba_sample_kernel(pt_ref, q_ref, kn_ref, vn_ref, expand_ref, pool_kt, pool_vt, o_ref,
                        buf, sem, kst_ref, m_ref, l_ref, acc_ref, *, pg, ng, n_blk, layer):
    b = pl.program_id(0)
    t = q_ref.shape[0]
    rows = A_HEADS * t
    per = MOBA_BLOCK // PAGE_SIZE
    bpg = pg // per

    def copies(seq, i):
        pool = pool_kt if i < ng else pool_vt
        return _page_copies(pt_ref, seq, (i % ng) * pg, pg, (pool,), (buf,), (sem,), layer, i % 2)

    def start(seq, i):
        _start_all(copies(seq, i))

    def wait(seq, i):
        _wait_all(copies(seq, i))

    @pl.when(b == 0)
    def _():
        start(b, 0)

    qm = _stack_heads_masked(q_ref[...], A_HEADS, A_HEAD_DIM)
    qb = (qm * (A_HEAD_DIM ** -0.5)).astype(BF16)

    lane = lax.broadcasted_iota(jnp.int32, (A_WIDTH, LANES), 1)
    kmean = jnp.zeros((A_WIDTH, LANES), F32)
    for g in range(ng):
        start(b, g + 1)
        wait(b, g)
        for i in range(bpg):
            blk_sum = buf[g % 2, i * per]
            for j in range(1, per):
                blk_sum = blk_sum + buf[g % 2, i * per + j]
            col = jnp.sum(blk_sum, axis=1, keepdims=True) * (1.0 / MOBA_BLOCK)
            kmean = jnp.where(lane == g * bpg + i, col, kmean)
        kst_ref[g] = jnp.concatenate([buf[g % 2, j].astype(BF16) for j in range(pg)], axis=1)

    gate = _doth(qm, kmean)
    blk = lax.broadcasted_iota(jnp.int32, gate.shape, 1)
    bias = jnp.where(_top_select(gate, blk < n_blk, 1), 0.0, NEG)

    _softmax_reset(m_ref, l_ref, acc_ref)
    s = _dot_nt(qb, kn_ref[...])
    qt = lax.broadcasted_iota(jnp.int32, (rows, t), 0) % t
    kt = lax.broadcasted_iota(jnp.int32, (rows, t), 1)
    _softmax_update(jnp.where(kt <= qt, s, NEG), m_ref, l_ref, acc_ref, vn_ref[...].astype(BF16), rows)

    r_i = lax.broadcasted_iota(jnp.int32, (LANES, LANES), 0)
    c_i = lax.broadcasted_iota(jnp.int32, (LANES, LANES), 1)
    for g in range(ng):
        if g + 1 < ng:
            start(b, ng + g + 1)
        else:
            @pl.when(b + 1 < pl.num_programs(0))
            def _():
                start(b + 1, 0)
        wait(b, ng + g)
        vc = jnp.concatenate([buf[(ng + g) % 2, j].astype(BF16) for j in range(pg)], axis=1)
        pick = ((r_i == g * bpg + c_i) & (c_i < bpg)).astype(BF16)
        bias_g = _dot(bias, pick)[:, :bpg].astype(BF16)
        s = (jnp.dot(qb, kst_ref[g], preferred_element_type=F32)
             + jnp.dot(bias_g, expand_ref[...], preferred_element_type=F32))
        _softmax_update(s, m_ref, l_ref, acc_ref, vc, rows, nt=True)

    o_ref[...] = _merge_heads(acc_ref[...] / l_ref[...], A_HEADS, A_HEAD_DIM)


def _moba_sample(q, k_new, v_new, pool_kt, pool_vt, page_table, layer, pg):
    db, t, w = q.shape
    n_pages = page_table.shape[1]
    n_blk = n_pages * PAGE_SIZE // MOBA_BLOCK
    assert n_blk <= LANES
    ng = n_pages // pg
    bpg = pg * PAGE_SIZE // MOBA_BLOCK
    rows = A_HEADS * t
    tok = pl.BlockSpec((None, t, w), lambda b, pt: (b, 0, 0))
    hbm = pl.BlockSpec(memory_space=pl.ANY)
    key_blk = jnp.arange(pg * PAGE_SIZE, dtype=jnp.int32) // MOBA_BLOCK
    expand = (jnp.arange(bpg, dtype=jnp.int32)[:, None] == key_blk[None, :]).astype(BF16)
    return pl.pallas_call(
        functools.partial(_moba_sample_kernel, pg=pg, ng=ng, n_blk=n_blk, layer=layer),
        grid_spec=pltpu.PrefetchScalarGridSpec(
            num_scalar_prefetch=1, grid=(db,),
            in_specs=[tok, tok, tok, pl.BlockSpec(expand.shape, lambda b, pt: (0, 0)), hbm, hbm],
            out_specs=tok,
            scratch_shapes=[pltpu.VMEM((2, pg, w, PAGE_SIZE), F32), pltpu.SemaphoreType.DMA((2,)),
                            pltpu.VMEM((ng, w, pg * PAGE_SIZE), BF16),
                            pltpu.VMEM((rows, 1), F32), pltpu.VMEM((rows, 1), F32),
                            pltpu.VMEM((rows, w), F32)]),
        out_shape=jax.ShapeDtypeStruct((db, t, w), F32),
        compiler_params=_cp(("arbitrary",)),
        name="moba_sample",
    )(page_table, q, k_new, v_new, expand, pool_kt, pool_vt)


def _mla_sample_kernel(pt_ref, qabs_ref, qrope_ref, cn_ref, rn_ref, wuv_ref, pool_c, pool_rt, o_ref,
                       cbuf, rbuf, csem, rsem, m_ref, l_ref, acc_ref, *, pg, layer):
    b, g = pl.program_id(0), pl.program_id(1)
    nb, ng = pl.num_programs(0), pl.num_programs(1)
    step = b * ng + g
    slot = step % 2
    t = qabs_ref.shape[0]
    rows = B_HEADS * t

    def group_copies(step_idx, slot_idx):
        return _page_copies(pt_ref, step_idx // ng, (step_idx % ng) * pg, pg, (pool_c, pool_rt),
                            (cbuf, rbuf), (csem, rsem), layer, slot_idx)

    @pl.when(step == 0)
    def _():
        _start_all(group_copies(step, slot))

    @pl.when(step + 1 < nb * ng)
    def _():
        _start_all(group_copies(step + 1, 1 - slot))

    qs, qr = _stack_mla_q(qabs_ref, qrope_ref)

    @pl.when(g == 0)
    def _():
        _softmax_reset(m_ref, l_ref, acc_ref)
        cn = cn_ref[...].astype(BF16)
        s = _dot_nt(qs, cn) + _dot_nt(qr, rn_ref[...])
        qt = lax.broadcasted_iota(jnp.int32, (rows, t), 0) % t
        kt = lax.broadcasted_iota(jnp.int32, (rows, t), 1)
        _softmax_update(jnp.where(kt <= qt, s, NEG), m_ref, l_ref, acc_ref, cn, rows)

    _wait_all(group_copies(step, slot))
    cc = cbuf[slot].reshape(pg * PAGE_SIZE, B_KV_RANK).astype(BF16)
    rc = jnp.concatenate([rbuf[slot, j].astype(BF16) for j in range(pg)], axis=1)
    s = _dot_nt(qs, cc) + jnp.dot(qr, rc, preferred_element_type=F32)
    _softmax_update(s, m_ref, l_ref, acc_ref, cc, rows)

    @pl.when(g == ng - 1)
    def _():
        _mla_finish(acc_ref[...], l_ref[...], wuv_ref, o_ref, t)


def _mla_sample(qabs, qrope, c_new, r_new, pool_c, pool_rt, page_table, wuv, layer, pg):
    db, t, _ = qabs.shape
    n_pages = page_table.shape[1]
    rows = B_HEADS * t
    tok = lambda w: pl.BlockSpec((None, t, w), lambda b, g, pt: (b, 0, 0))
    hbm = pl.BlockSpec(memory_space=pl.ANY)
    return pl.pallas_call(
        functools.partial(_mla_sample_kernel, pg=pg, layer=layer),
        grid_spec=pltpu.PrefetchScalarGridSpec(
            num_scalar_prefetch=1, grid=(db, n_pages // pg),
            in_specs=[tok(qabs.shape[2]), tok(qrope.shape[2]), tok(B_KV_RANK), tok(B_ROPE_DIM),
                      pl.BlockSpec(wuv.shape, lambda b, g, pt: (0, 0, 0)), hbm, hbm],
            out_specs=tok(B_WIDTH),
            scratch_shapes=[pltpu.VMEM((2, pg, PAGE_SIZE, B_KV_RANK), F32),
                            pltpu.VMEM((2, pg, B_ROPE_DIM, PAGE_SIZE), F32),
                            pltpu.SemaphoreType.DMA((2,)), pltpu.SemaphoreType.DMA((2,)),
                            pltpu.VMEM((rows, 1), F32), pltpu.VMEM((rows, 1), F32),
                            pltpu.VMEM((rows, B_KV_RANK), F32)]),
        out_shape=jax.ShapeDtypeStruct((db, t, B_WIDTH), F32),
        compiler_params=_cp(("arbitrary", "arbitrary")),
        name="mla_sample",
    )(page_table, qabs, qrope, c_new, r_new, wuv, pool_c, pool_rt)


def _prep_w_in(w):
    z = jnp.zeros((w.shape[0], 64), w.dtype)
    return jnp.concatenate(
        [w[:, 0:1536], w[:, 1600:2112], w[:, 2112:2880], w[:, 3008:3264], w[:, 1536:1600], z,
         w[:, 2880:3008]], axis=1).astype(BF16)


def _prep_w_uq(w):
    w = w.reshape(B_Q_RANK, B_HEADS, B_NOPE_DIM + B_ROPE_DIM)
    nope = w[:, :, :B_NOPE_DIM].reshape(B_Q_RANK, B_HEADS * B_NOPE_DIM)
    rope = jnp.pad(w[:, :, B_NOPE_DIM:], ((0, 0), (0, 0), (0, LANES - B_ROPE_DIM)))
    return jnp.concatenate([nope, rope.reshape(B_Q_RANK, B_HEADS * LANES)], axis=1).astype(BF16)


def _pad_rows_64(w, top):
    z = jnp.zeros((64, w.shape[1]), w.dtype)
    return jnp.concatenate([w, z] if top else [z, w], axis=0)


def kernel(x_prompt, x_sample, cache_moba_k, cache_moba_v, cache_mla_latent, cache_mla_rope, state_rwkv_wkv, state_rwkv_shift, page_table, c_prompt, c_sample, w_ada, b_ada, w_in, mla_q_norm, mla_kv_norm, mla_w_uq, mla_w_uk, mla_w_uv, rwkv_mu, rwkv_w0, rwkv_w2, rwkv_a0, rwkv_a2, rwkv_k_k, rwkv_k_a, rwkv_r_k, rwkv_gn_g, rwkv_gn_b, w_out, ln_g, ln_b):
    depth = w_in.shape[0]
    bp, sp, d = x_prompt.shape
    db, ts, _ = x_sample.shape
    n_pages = page_table.shape[1]
    n_pool = cache_moba_k.shape[1]
    past = n_pages * PAGE_SIZE
    alpha = (2 * depth) ** 0.25
    tm_p = min(256, sp)
    rs = db * ts
    tm_s = rs
    pg = min(32, n_pages)

    mods = _ada_all(jnp.concatenate([c_prompt, c_sample], axis=0), w_ada, b_ada)

    pos_p = jnp.arange(sp, dtype=jnp.int32)
    pos_s = jnp.tile(past + jnp.arange(ts, dtype=jnp.int32), db)
    tabs = [(_rope_tables(p, A_ROPE_DIM // 2, ROPE_THETA, A_HEAD_DIM, A_HEADS),
             _rope_tables(p, B_ROPE_DIM // 2, MLA_ROPE_THETA, LANES, 1)) for p in (pos_p, pos_s)]

    pool_kt = jnp.transpose(cache_moba_k, (0, 1, 3, 4, 2)).reshape(depth, n_pool, A_WIDTH, PAGE_SIZE)
    pool_vt = jnp.transpose(cache_moba_v, (0, 1, 3, 4, 2)).reshape(depth, n_pool, A_WIDTH, PAGE_SIZE)
    pool_rt = jnp.transpose(cache_mla_rope, (0, 1, 3, 2))

    xp = x_prompt.reshape(bp * sp, d)
    xs = x_sample.reshape(rs, d)
    zero_state = jnp.zeros((bp, C_WIDTH, C_WIDTH), F32)
    outs = [[] for _ in range(12)]

    for l in range(depth):
        w = _prep_w_in(w_in[l])
        wuq = _prep_w_uq(mla_w_uq[l])
        wuk = jnp.transpose(mla_w_uk[l], (1, 2, 0)).astype(BF16)
        wuv = jnp.transpose(mla_w_uv[l], (1, 0, 2)).astype(BF16)
        qn = mla_q_norm[l].reshape(1, -1)
        kvn = mla_kv_norm[l].reshape(1, -1)
        mu = rwkv_mu[l]
        row = lambda a: a.reshape(1, -1)
        rwkv_params = (row(mu[:3 * C_WIDTH]), row(mu[3 * C_WIDTH:]), row(rwkv_w0[l]),
                       _pad_rows_64(rwkv_w2[l], True).astype(BF16), row(rwkv_a0[l]),
                       _pad_rows_64(rwkv_a2[l], False).astype(BF16), row(rwkv_k_k[l]),
                       row(rwkv_k_a[l]), row(rwkv_r_k[l]), row(rwkv_gn_g[l]), row(rwkv_gn_b[l]))
        wo = w_out[l].astype(BF16)
        lng, lnb = row(ln_g[l]), row(ln_b[l])

        for stream in (0, 1):
            if stream == 0:
                x2d, nb_, t_, tm = xp, bp, sp, tm_p
                mod = mods[l, :bp].reshape(bp, 1, 3 * d)
                tiles_per_mod = sp // tm
                prev = jnp.zeros((bp, 3 * C_WIDTH + LANES), F32)
                s0 = zero_state
            else:
                x2d, nb_, t_, tm = xs, db, ts, tm_s
                mod = jnp.repeat(mods[l, bp:], ts, axis=0).reshape(1, rs, 3 * d)
                tiles_per_mod = 1
                prev = state_rwkv_shift[l]
                s0 = _state_to_bd(state_rwkv_wkv[l])
            shift, scale, gate = mod[..., :d], mod[..., d:2 * d], mod[..., 2 * d:]
            tab_a, tab_b = tabs[stream]
            (qa, ka, va, ga, qabs, qrope, ckv, kr, gb, rkv, wdad, gc) = _in_proj(
                x2d, scale, shift, tiles_per_mod, w, wuq, wuk, qn, kvn, tab_a, tab_b, tm)
            r3 = lambda a: a.reshape(nb_, t_, a.shape[-1])
            if stream == 0:
                oa, ob = _prompt_attn(r3(qa), r3(ka), r3(va), r3(qabs), r3(qrope), r3(ckv), r3(kr), wuv)
            else:
                oa = _moba_sample(r3(qa), r3(ka), r3(va), pool_kt, pool_vt, page_table, l, pg)
                ob = _mla_sample(r3(qabs), r3(qrope), r3(ckv), r3(kr), cache_mla_latent,
                                 pool_rt, page_table, wuv, l, pg)
            oc, s_new = _rwkv(r3(rkv), r3(wdad), prev[:, None, :3 * C_WIDTH],
                              prev[:, None, 3 * C_WIDTH:], s0, rwkv_params)
            y = _out_proj(x2d, gate, tiles_per_mod, oa.reshape(-1, A_WIDTH), ga,
                          ob.reshape(-1, B_WIDTH), gb, oc.reshape(-1, C_WIDTH), gc, wo, lng, lnb,
                          alpha, tm)
            last = jnp.concatenate([r3(rkv)[:, -1], r3(wdad)[:, -1]], axis=-1)
            leaves = (ka.reshape(nb_, t_, A_HEADS, A_HEAD_DIM), va.reshape(nb_, t_, A_HEADS, A_HEAD_DIM),
                      r3(ckv), r3(kr), _state_from_bd(s_new), last)
            for i, leaf in enumerate(leaves):
                outs[stream * 6 + i].append(leaf)
            if stream == 0:
                xp = y
            else:
                xs = y

    return (xp.reshape(bp, sp, d), xs.reshape(db, ts, d)) + tuple(jnp.stack(o) for o in outs)
```
